```python
import math, functools
import jax, jax.numpy as jnp
from jax import lax
import numpy as np


D_MODEL = 4096
BATCH = 4
SEQ = 2048
DEPTH = 4
DEC_BATCH = 8
DEC_SEQ = 4
PAST_LEN = 8192
PAGE_SIZE = 128

MIX_WIDTH = D_MODEL
ATTN_WIDTH = MIX_WIDTH // 2
N_HEADS = 8
HEAD_DIM = ATTN_WIDTH // (2 * N_HEADS)
V_DIM = 2 * HEAD_DIM
SSM_WIDTH = MIX_WIDTH // 4
SSM_CH = 16
SSM_GROUPS = SSM_WIDTH // SSM_CH
SSM_STATE = 64
POOL_WIDTH = MIX_WIDTH - ATTN_WIDTH - SSM_WIDTH
POOL_WINDOWS = (2, 4, 8, 16)
POOL_GROUP = POOL_WIDTH // len(POOL_WINDOWS)
POOL_BUF = max(POOL_WINDOWS) - 1
D_FF = 256 * ((8 * D_MODEL // 3 + 255) // 256)
N_IN = 3 * ATTN_WIDTH + SSM_WIDTH + POOL_WIDTH
NUM_BUCKETS = 32
MAX_DISTANCE = 128
Q_BLOCK = 128
RMS_EPS = 1e-6
N_NORMS = 6

kernel_name = 'hybrid_diffattn_s5_pool_decoder_step'


def rmsnorm(x, g):
    xf = x.astype(jnp.float32)
    xf = xf * lax.rsqrt(jnp.mean(xf * xf, axis=-1, keepdims=True) + RMS_EPS)
    return (xf * g.astype(jnp.float32)).astype(x.dtype)


def ffn_sublayer(x, g_pre, g_post, w_gate, w_up, w_down):
    h = rmsnorm(x, g_pre)
    f = (jax.nn.silu(h @ w_gate) * (h @ w_up)) @ w_down
    return x + 0.5 * rmsnorm(f, g_post)


def t5_bucket(rel):
    n = jnp.maximum(rel, 0)
    max_exact = NUM_BUCKETS // 2
    nf = jnp.maximum(n, 1).astype(jnp.float32)
    large = max_exact + (jnp.log(nf / max_exact) / math.log(MAX_DISTANCE / max_exact)
                         * (NUM_BUCKETS - max_exact)).astype(jnp.int32)
    large = jnp.minimum(large, NUM_BUCKETS - 1)
    return jnp.where(n < max_exact, n, large)


def diff_lambda_value(lp, lam_init):
    lp = lp.astype(jnp.float32)
    return jnp.exp(jnp.sum(lp[0] * lp[1])) - jnp.exp(jnp.sum(lp[2] * lp[3])) + lam_init


def diff_attn_core(q, k, v, q_pos, k_pos, rel_bias, lam, lam_init, subln_g):
    scores = jnp.einsum('bqhcd,bkhcd->bchqk', q, k).astype(jnp.float32) * (HEAD_DIM ** -0.5)
    bias = rel_bias[t5_bucket(q_pos[:, None] - k_pos[None, :])].astype(jnp.float32)
    bias = jnp.transpose(bias, (2, 0, 1))
    causal = k_pos[None, :] <= q_pos[:, None]
    scores = jnp.where(causal, scores + bias, -jnp.inf)
    probs = jax.nn.softmax(scores, axis=-1)
    weights = probs[:, 0] - lam * probs[:, 1]
    out = jnp.einsum('bhqk,bkhe->bqhe', weights.astype(v.dtype), v)
    out = rmsnorm(out, subln_g) * (1.0 - lam_init)
    b, nq = q.shape[:2]
    return out.reshape(b, nq, N_HEADS * V_DIM)


def diff_attn_prompt(q, k, v, rel_bias, lam, lam_init, subln_g):
    b, s = q.shape[:2]
    nb = s // Q_BLOCK
    q_blocks = jnp.swapaxes(q.reshape(b, nb, Q_BLOCK, N_HEADS, 2, HEAD_DIM), 0, 1)
    k_pos = jnp.arange(s)

    def one_block(args):
        i, qb = args
        q_pos = i * Q_BLOCK + jnp.arange(Q_BLOCK)
        return diff_attn_core(qb, k, v, q_pos, k_pos, rel_bias, lam, lam_init, subln_g)

    out = lax.map(one_block, (jnp.arange(nb), q_blocks))
    return jnp.swapaxes(out, 0, 1).reshape(b, s, N_HEADS * V_DIM)


def diff_attn_sample(q, k, v, k_past, v_past, past_len, rel_bias, lam, lam_init, subln_g):
    nq = q.shape[1]
    k_all = jnp.concatenate([k_past.astype(k.dtype), k], axis=1)
    v_all = jnp.concatenate([v_past.astype(v.dtype), v], axis=1)
    q_pos = past_len + jnp.arange(nq)
    k_pos = jnp.arange(past_len + nq)
    return diff_attn_core(q, k_all, v_all, q_pos, k_pos, rel_bias, lam, lam_init, subln_g)


def _lin_combine(e1, e2):
    a1, b1 = e1
    a2, b2 = e2
    return (a1 * a2, a2 * b1 + b2)


def ssm_mix(u, s0_re, s0_im, lam_re, lam_im, log_dt, b_re, b_im, c_re, c_im, d_skip, w_glu):
    f32 = jnp.float32
    b, l, _ = u.shape
    uf = u.astype(f32)
    lam = lax.complex(lam_re.astype(f32), lam_im.astype(f32))
    dt = jnp.exp(log_dt.astype(f32))[:, None]
    lam_bar = jnp.exp(lam * dt)
    b_bar = ((lam_bar - 1.0) / lam)[..., None] * lax.complex(b_re.astype(f32), b_im.astype(f32))
    c = lax.complex(c_re.astype(f32), c_im.astype(f32))
    ug = uf.reshape(b, l, SSM_GROUPS, SSM_CH).astype(jnp.complex64)
    bu = jnp.einsum('gpc,blgc->blgp', b_bar, ug)
    a = jnp.broadcast_to(lam_bar, bu.shape)
    a_cum, s = lax.associative_scan(_lin_combine, (a, bu), axis=1)
    s = s + a_cum * lax.complex(s0_re.astype(f32), s0_im.astype(f32))[:, None]
    y = jnp.einsum('gcp,blgp->blgc', c, s).real.reshape(b, l, SSM_WIDTH) + d_skip.astype(f32) * uf
    g = jax.nn.gelu(y)
    out = g * jax.nn.sigmoid(g @ w_glu.astype(f32))
    s_last = s[:, -1]
    return out.astype(u.dtype), jnp.real(s_last), jnp.imag(s_last)


def pool_mix(u, prefix, n_prev, pool_w, pool_scale):
    f32 = jnp.float32
    b, l, _ = u.shape
    uf = u.astype(f32)
    xp = jnp.concatenate([prefix.astype(f32), uf], axis=1)
    csum = jnp.concatenate([jnp.zeros((b, 1, POOL_WIDTH), f32), jnp.cumsum(xp, axis=1)], axis=1)
    t = jnp.arange(1, l + 1)
    outs = []
    for gi, win in enumerate(POOL_WINDOWS):
        ch = slice(gi * POOL_GROUP, (gi + 1) * POOL_GROUP)
        hi = csum[:, POOL_BUF + 1:POOL_BUF + 1 + l, ch]
        lo = csum[:, POOL_BUF + 1 - win:POOL_BUF + 1 - win + l, ch]
        count = jnp.minimum(n_prev + t, win).astype(f32)[None, :, None]
        outs.append(((hi - lo) / count - uf[..., ch]) @ pool_w[gi].astype(f32))
    y = jnp.concatenate(outs, axis=-1) * pool_scale.astype(f32)
    return y.astype(u.dtype), xp[:, -POOL_BUF:].astype(u.dtype)


def mixer_sublayer(x, g_pre, g_post, w_in, w_out, attn_fn, s0_re, s0_im, pool_prefix, n_prev,
                   ssm_params, pool_w, pool_scale):
    b, l, _ = x.shape
    h = rmsnorm(x, g_pre) @ w_in
    q = h[..., :ATTN_WIDTH].reshape(b, l, N_HEADS, 2, HEAD_DIM)
    k = h[..., ATTN_WIDTH:2 * ATTN_WIDTH].reshape(b, l, N_HEADS, 2, HEAD_DIM)
    v = h[..., 2 * ATTN_WIDTH:3 * ATTN_WIDTH].reshape(b, l, N_HEADS, V_DIM)
    us = h[..., 3 * ATTN_WIDTH:3 * ATTN_WIDTH + SSM_WIDTH]
    up = h[..., 3 * ATTN_WIDTH + SSM_WIDTH:]
    a_out = attn_fn(q, k, v)
    s_out, s_re, s_im = ssm_mix(us, s0_re, s0_im, *ssm_params)
    p_out, p_buf = pool_mix(up, pool_prefix, n_prev, pool_w, pool_scale)
    mixed = jnp.concatenate([a_out, s_out, p_out], axis=-1) @ w_out
    return x + rmsnorm(mixed, g_post), (k, v, s_re, s_im, p_buf)


def setup_inputs(seed: int = 0) -> dict:
    key = jax.random.key(seed)
    ks = jax.random.split(key, 32)
    f32 = jnp.float32

    def nrm(k, shape, scale):
        return scale * jax.random.normal(k, shape, f32)

    n_pages = PAST_LEN // PAGE_SIZE
    n_used = DEC_BATCH * n_pages
    n_pool = n_used + (n_used + 3) // 4
    page_table = jax.random.permutation(ks[0], n_pool)[:n_used].reshape(DEC_BATCH, n_pages).astype(jnp.int32)
    n_idx = jnp.arange(SSM_STATE, dtype=f32)
    return {
        'x_prompt': nrm(ks[1], (BATCH, SEQ, D_MODEL), 1.0),
        'x_sample': nrm(ks[2], (DEC_BATCH, DEC_SEQ, D_MODEL), 1.0),
        'cache_k': nrm(ks[3], (DEPTH, n_pool, PAGE_SIZE, N_HEADS, 2, HEAD_DIM), 1.0),
        'cache_v': nrm(ks[4], (DEPTH, n_pool, PAGE_SIZE, N_HEADS, V_DIM), 1.0),
        'state_ssm_re': nrm(ks[5], (DEPTH, DEC_BATCH, SSM_GROUPS, SSM_STATE), 0.5),
        'state_ssm_im': nrm(ks[6], (DEPTH, DEC_BATCH, SSM_GROUPS, SSM_STATE), 0.5),
        'state_pool': nrm(ks[7], (DEPTH, DEC_BATCH, POOL_BUF, POOL_WIDTH), 1.0),
        'page_table': page_table,
        'norm_g': 1.0 + nrm(ks[8], (DEPTH, N_NORMS, D_MODEL), 0.02),
        'w_ffn_gate': nrm(ks[9], (DEPTH, 2, D_MODEL, D_FF), D_MODEL ** -0.5),
        'w_ffn_up': nrm(ks[10], (DEPTH, 2, D_MODEL, D_FF), D_MODEL ** -0.5),
        'w_ffn_down': nrm(ks[11], (DEPTH, 2, D_FF, D_MODEL), D_FF ** -0.5),
        'w_in': nrm(ks[12], (DEPTH, D_MODEL, N_IN), D_MODEL ** -0.5),
        'w_out': nrm(ks[13], (DEPTH, MIX_WIDTH, D_MODEL), MIX_WIDTH ** -0.5),
        'rel_bias': nrm(ks[14], (NUM_BUCKETS, N_HEADS), 0.5),
        'diff_lambda': nrm(ks[15], (DEPTH, 4, HEAD_DIM), 0.1),
        'diff_subln': 1.0 + nrm(ks[16], (DEPTH, V_DIM), 0.02),
        'ssm_lam_re': -0.5 + nrm(ks[17], (DEPTH, SSM_GROUPS, SSM_STATE), 0.01),
        'ssm_lam_im': math.pi * n_idx + nrm(ks[18], (DEPTH, SSM_GROUPS, SSM_STATE), 0.01),
        'ssm_log_dt': jax.random.uniform(ks[19], (DEPTH, SSM_GROUPS), f32, math.log(1e-3), math.log(1e-1)),
        'ssm_b_re': nrm(ks[20], (DEPTH, SSM_GROUPS, SSM_STATE, SSM_CH), (2 * SSM_CH) ** -0.5),
        'ssm_b_im': nrm(ks[21], (DEPTH, SSM_GROUPS, SSM_STATE, SSM_CH), (2 * SSM_CH) ** -0.5),
        'ssm_c_re': nrm(ks[22], (DEPTH, SSM_GROUPS, SSM_CH, SSM_STATE), (2 * SSM_STATE) ** -0.5),
        'ssm_c_im': nrm(ks[23], (DEPTH, SSM_GROUPS, SSM_CH, SSM_STATE), (2 * SSM_STATE) ** -0.5),
        'ssm_d': nrm(ks[24], (DEPTH, SSM_WIDTH), 1.0),
        'ssm_w_glu': nrm(ks[25], (DEPTH, SSM_WIDTH, SSM_WIDTH), SSM_WIDTH ** -0.5),
        'pool_w': nrm(ks[26], (DEPTH, len(POOL_WINDOWS), POOL_GROUP, POOL_GROUP), POOL_GROUP ** -0.5),
        'pool_scale': 1.0 + nrm(ks[27], (DEPTH, POOL_WIDTH), 0.02),
    }


def reference(x_prompt, x_sample, cache_k, cache_v, state_ssm_re, state_ssm_im, state_pool, page_table,
              norm_g, w_ffn_gate, w_ffn_up, w_ffn_down, w_in, w_out, rel_bias, diff_lambda, diff_subln,
              ssm_lam_re, ssm_lam_im, ssm_log_dt, ssm_b_re, ssm_b_im, ssm_c_re, ssm_c_im, ssm_d, ssm_w_glu,
              pool_w, pool_scale):
    bp = x_prompt.shape[0]
    db = x_sample.shape[0]
    past_len = page_table.shape[1] * PAGE_SIZE
    zeros_re = jnp.zeros((bp, SSM_GROUPS, SSM_STATE), jnp.float32)
    zeros_prefix = jnp.zeros((bp, POOL_BUF, POOL_WIDTH), x_prompt.dtype)
    xp, xs = x_prompt, x_sample
    kp_l, vp_l, srp_l, sip_l, pp_l = [], [], [], [], []
    ks_l, vs_l, srs_l, sis_l, ps_l = [], [], [], [], []
    for l in range(DEPTH):
        lam_init = 0.8 - 0.6 * math.exp(-0.3 * l)
        lam = diff_lambda_value(diff_lambda[l], lam_init)
        ssm_params = (ssm_lam_re[l], ssm_lam_im[l], ssm_log_dt[l], ssm_b_re[l], ssm_b_im[l],
                      ssm_c_re[l], ssm_c_im[l], ssm_d[l], ssm_w_glu[l])
        g = norm_g[l]
        attn_p = functools.partial(diff_attn_prompt, rel_bias=rel_bias, lam=lam, lam_init=lam_init,
                                   subln_g=diff_subln[l])
        xp = ffn_sublayer(xp, g[0], g[1], w_ffn_gate[l, 0], w_ffn_up[l, 0], w_ffn_down[l, 0])
        xp, (kp, vp, srp, sip, pp) = mixer_sublayer(xp, g[2], g[3], w_in[l], w_out[l], attn_p, zeros_re, zeros_re,
                                                    zeros_prefix, 0, ssm_params, pool_w[l], pool_scale[l])
        xp = ffn_sublayer(xp, g[4], g[5], w_ffn_gate[l, 1], w_ffn_up[l, 1], w_ffn_down[l, 1])
        kp_l.append(kp); vp_l.append(vp); srp_l.append(srp); sip_l.append(sip); pp_l.append(pp)
        k_past = cache_k[l][page_table].reshape(db, past_len, N_HEADS, 2, HEAD_DIM)
        v_past = cache_v[l][page_table].reshape(db, past_len, N_HEADS, V_DIM)
        attn_s = functools.partial(diff_attn_sample, k_past=k_past, v_past=v_past, past_len=past_len,
                                   rel_bias=rel_bias, lam=lam, lam_init=lam_init, subln_g=diff_subln[l])
        xs = ffn_sublayer(xs, g[0], g[1], w_ffn_gate[l, 0], w_ffn_up[l, 0], w_ffn_down[l, 0])
        xs, (ksn, vsn, srs, sis, ps) = mixer_sublayer(xs, g[2], g[3], w_in[l], w_out[l], attn_s,
                                                      state_ssm_re[l], state_ssm_im[l], state_pool[l], past_len,
                                                      ssm_params, pool_w[l], pool_scale[l])
        xs = ffn_sublayer(xs, g[4], g[5], w_ffn_gate[l, 1], w_ffn_up[l, 1], w_ffn_down[l, 1])
        ks_l.append(ksn); vs_l.append(vsn); srs_l.append(srs.astype(state_ssm_re.dtype))
        sis_l.append(sis.astype(state_ssm_im.dtype)); ps_l.append(ps)
    return (xp, xs,
            jnp.stack(kp_l), jnp.stack(vp_l), jnp.stack(srp_l), jnp.stack(sip_l), jnp.stack(pp_l),
            jnp.stack(ks_l), jnp.stack(vs_l), jnp.stack(srs_l), jnp.stack(sis_l), jnp.stack(ps_l))
```

```python
import functools
import math

import jax
import jax.numpy as jnp
from jax import lax
from jax.experimental import pallas as pl
from jax.experimental.pallas import tpu as pltpu

F32 = jnp.float32
BF16 = jnp.bfloat16

RMS_EPS = 1e-6
N_HEADS = 8
NUM_BUCKETS = 32
MAX_DISTANCE = 128
POOL_WINDOWS = (2, 4, 8, 16)
POOL_HIST = 16
SSM_CH = 16
LANE = 128
SUBLANE = 8
VMEM_LIMIT = 56 * 1024 * 1024


def _tile(dim, pref, align):
    t = min(pref, dim)
    t -= t % align
    while t >= align:
        if dim % t == 0:
            return t
        t -= align
    return dim


def _params(*sem):
    return pltpu.CompilerParams(dimension_semantics=sem, vmem_limit_bytes=VMEM_LIMIT)


def _rmsnorm_kernel(x_ref, g_ref, o_ref):
    x = x_ref[...]
    ms = jnp.mean(x * x, axis=-1, keepdims=True)
    o_ref[...] = (x * lax.rsqrt(ms + RMS_EPS) * g_ref[...]).astype(o_ref.dtype)


def rmsnorm_cast(x, g):
    m, d = x.shape
    tm = _tile(m, 256, SUBLANE)
    return pl.pallas_call(
        _rmsnorm_kernel,
        grid=(m // tm,),
        in_specs=[pl.BlockSpec((tm, d), lambda i: (i, 0)),
                  pl.BlockSpec((1, d), lambda i: (0, 0))],
        out_specs=pl.BlockSpec((tm, d), lambda i: (i, 0)),
        out_shape=jax.ShapeDtypeStruct((m, d), BF16),
        compiler_params=_params("parallel"),
        name="rmsnorm_cast",
    )(x, g.reshape(1, d))


def _resnorm_kernel(x_ref, y_ref, g_ref, o_ref, *, scale):
    y = y_ref[...]
    ms = jnp.mean(y * y, axis=-1, keepdims=True)
    o_ref[...] = x_ref[...] + scale * (y * lax.rsqrt(ms + RMS_EPS) * g_ref[...])


def resnorm(x, y, g, scale):
    m, d = x.shape
    tm = _tile(m, 256, SUBLANE)
    return pl.pallas_call(
        functools.partial(_resnorm_kernel, scale=scale),
        grid=(m // tm,),
        in_specs=[pl.BlockSpec((tm, d), lambda i: (i, 0)),
                  pl.BlockSpec((tm, d), lambda i: (i, 0)),
                  pl.BlockSpec((1, d), lambda i: (0, 0))],
        out_specs=pl.BlockSpec((tm, d), lambda i: (i, 0)),
        out_shape=jax.ShapeDtypeStruct((m, d), F32),
        compiler_params=_params("parallel"),
        name="resnorm",
    )(x, y, g.reshape(1, d))


def _gateup_kernel(a_ref, wg_ref, wu_ref, o_ref):
    a = a_ref[...]
    g = jnp.dot(a, wg_ref[...].astype(BF16), preferred_element_type=F32)
    u = jnp.dot(a, wu_ref[...].astype(BF16), preferred_element_type=F32)
    o_ref[...] = (g * jax.nn.sigmoid(g) * u).astype(o_ref.dtype)


def gateup(a, wg, wu):
    m, k = a.shape
    n = wg.shape[1]
    tm = _tile(m, 1024, 16)
    tn = _tile(n, 256, LANE)
    return pl.pallas_call(
        _gateup_kernel,
        grid=(m // tm, n // tn),
        in_specs=[pl.BlockSpec((tm, k), lambda i, j: (i, 0)),
                  pl.BlockSpec((k, tn), lambda i, j: (0, j)),
                  pl.BlockSpec((k, tn), lambda i, j: (0, j))],
        out_specs=pl.BlockSpec((None, tm, tn), lambda i, j: (j, i, 0)),
        out_shape=jax.ShapeDtypeStruct((n // tn, m, tn), BF16),
        compiler_params=_params("parallel", "arbitrary"),
        name="gateup",
    )(a, wg, wu)


def _down_kernel(a_ref, w_ref, o_ref, *, n_chunk):
    k = pl.program_id(1)
    a = a_ref[...]
    n = o_ref.shape[1]

    @pl.when(k == 0)
    def _first():
        for c in range(0, n, n_chunk):
            o_ref[:, c:c + n_chunk] = jnp.dot(a, w_ref[:, c:c + n_chunk].astype(BF16),
                                              preferred_element_type=F32)

    @pl.when(k > 0)
    def _rest():
        for c in range(0, n, n_chunk):
            o_ref[:, c:c + n_chunk] += jnp.dot(a, w_ref[:, c:c + n_chunk].astype(BF16),
                                               preferred_element_type=F32)


def down(act, w):
    nk, m, tk = act.shape
    n = w.shape[1]
    tm = _tile(m, 1024, 16)
    n_chunk = _tile(n, 512, LANE)
    return pl.pallas_call(
        functools.partial(_down_kernel, n_chunk=n_chunk),
        grid=(m // tm, nk),
        in_specs=[pl.BlockSpec((None, tm, tk), lambda i, kk: (kk, i, 0)),
                  pl.BlockSpec((tk, n), lambda i, kk: (kk, 0))],
        out_specs=pl.BlockSpec((tm, n), lambda i, kk: (i, 0)),
        out_shape=jax.ShapeDtypeStruct((m, n), F32),
        compiler_params=_params("parallel", "arbitrary"),
        name="down",
    )(act, w)


def _mm_kernel(a_ref, w_ref, o_ref):
    o_ref[...] = jnp.dot(a_ref[...], w_ref[...].astype(BF16), preferred_element_type=F32)


def matmul(a, w):
    m, k = a.shape
    n = w.shape[1]
    tm = _tile(m, 1024, 16)
    tn = _tile(n, 512, LANE)
    return pl.pallas_call(
        _mm_kernel,
        grid=(m // tm, n // tn),
        in_specs=[pl.BlockSpec((tm, k), lambda i, j: (i, 0)),
                  pl.BlockSpec((k, tn), lambda i, j: (0, j))],
        out_specs=pl.BlockSpec((tm, tn), lambda i, j: (i, j)),
        out_shape=jax.ShapeDtypeStruct((m, n), F32),
        compiler_params=_params("parallel", "arbitrary"),
        name="matmul",
    )(a, w)


def _t5_bias(rel, rb_ref, h):
    n = jnp.maximum(rel, 0)
    max_exact = NUM_BUCKETS // 2
    nf = jnp.maximum(n, 1).astype(F32)
    large = max_exact + (jnp.log(nf / max_exact) / math.log(MAX_DISTANCE / max_exact)
                         * (NUM_BUCKETS - max_exact)).astype(jnp.int32)
    large = jnp.minimum(large, NUM_BUCKETS - 1)
    bucket = jnp.where(n < max_exact, n, large)
    out = jnp.zeros(rel.shape, F32)
    for b in range(NUM_BUCKETS):
        out = jnp.where(bucket == b, rb_ref[b, h], out)
    return out


def _bias_kernel(rb_ref, d_ref, dec_ref, *, t, dec_seq):
    h = pl.program_id(0)
    r = lax.broadcasted_iota(jnp.int32, (t, t), 0)
    c = lax.broadcasted_iota(jnp.int32, (t, t), 1)
    d_ref[0] = jnp.where(r >= c, _t5_bias(r - c, rb_ref, h), -jnp.inf)
    d_ref[1] = _t5_bias(t + r - c, rb_ref, h)
    row = lax.broadcasted_iota(jnp.int32, (SUBLANE, LANE), 0)
    col = lax.broadcasted_iota(jnp.int32, (SUBLANE, LANE), 1)
    tq = row % dec_seq
    dec_ref[0] = _t5_bias(jnp.full((SUBLANE, LANE), 2 * MAX_DISTANCE, jnp.int32), rb_ref, h)
    dec_ref[1] = _t5_bias(LANE + tq - col, rb_ref, h)
    dec_ref[2] = jnp.where((col <= tq) & (col < dec_seq), _t5_bias(tq - col, rb_ref, h), -jnp.inf)


def bias_tables(rel_bias, t, dec_seq):
    nb, nh = rel_bias.shape
    return pl.pallas_call(
        functools.partial(_bias_kernel, t=t, dec_seq=dec_seq),
        grid=(nh,),
        in_specs=[pl.BlockSpec(memory_space=pltpu.SMEM)],
        out_specs=[pl.BlockSpec((None, 2, t, t), lambda h: (h, 0, 0, 0)),
                   pl.BlockSpec((None, 3, SUBLANE, LANE), lambda h: (h, 0, 0, 0))],
        out_shape=[jax.ShapeDtypeStruct((nh, 2, t, t), F32),
                   jax.ShapeDtypeStruct((nh, 3, SUBLANE, LANE), F32)],
        compiler_params=_params("arbitrary"),
        name="bias_tables",
    )(rel_bias)


def _lambda_kernel(lp_ref, init_ref, o_ref):
    lp = lp_ref[...]
    s1 = jnp.sum(lp[:, 0, :] * lp[:, 1, :], axis=-1, keepdims=True)
    s2 = jnp.sum(lp[:, 2, :] * lp[:, 3, :], axis=-1, keepdims=True)
    o_ref[...] = jnp.broadcast_to(jnp.exp(s1) - jnp.exp(s2) + init_ref[...], o_ref.shape)


def diff_lambdas(diff_lambda, lam_init):
    depth = diff_lambda.shape[0]
    out = pl.pallas_call(
        _lambda_kernel,
        out_shape=jax.ShapeDtypeStruct((depth, LANE), F32),
        name="diff_lambdas",
    )(diff_lambda, jnp.asarray(lam_init, F32).reshape(depth, 1))
    return out[:, 0]


def _attn_kernel(par_ref, far_ref, q_ref, k_ref, v_ref, d_ref, g_ref, o_ref,
                 m_scr, l_scr, acc_scr, *, scale, hd):
    h = pl.program_id(1)
    qi = pl.program_id(2)
    ki = pl.program_id(3)

    @pl.when(ki == 0)
    def _init():
        m_scr[...] = jnp.full(m_scr.shape, -jnp.inf, F32)
        l_scr[...] = jnp.zeros(l_scr.shape, F32)
        acc_scr[...] = jnp.zeros(acc_scr.shape, F32)

    def step(bias):
        q = q_ref[...].astype(BF16)
        k = k_ref[...].astype(BF16)
        v = v_ref[...].astype(BF16)
        for c in range(2):
            s = lax.dot_general(q[:, c * hd:(c + 1) * hd], k[:, c * hd:(c + 1) * hd],
                                (((1,), (1,)), ((), ())), preferred_element_type=F32)
            s = s * scale + bias
            m_prev = m_scr[c]
            m_new = jnp.maximum(m_prev, jnp.max(s, axis=-1, keepdims=True))
            alpha = jnp.exp(m_prev - m_new)
            p = jnp.exp(s - m_new)
            l_scr[c] = alpha * l_scr[c] + jnp.sum(p, axis=-1, keepdims=True)
            acc_scr[c] = alpha * acc_scr[c] + jnp.dot(p.astype(BF16), v, preferred_element_type=F32)
            m_scr[c] = m_new

    @pl.when(ki == qi)
    def _diag():
        step(d_ref[0])

    @pl.when(ki == qi - 1)
    def _sub():
        step(d_ref[1])

    @pl.when(ki < qi - 1)
    def _far():
        step(far_ref[h])

    @pl.when(ki == qi)
    def _finish():
        o = acc_scr[0] / l_scr[0] - par_ref[0] * (acc_scr[1] / l_scr[1])
        ms = jnp.mean(o * o, axis=-1, keepdims=True)
        o_ref[...] = ((o * lax.rsqrt(ms + RMS_EPS) * g_ref[...]) * par_ref[1]).astype(o_ref.dtype)


def attn_prompt(h3, par, far, d_tiles, subln_g, t):
    b, s, _ = h3.shape
    nh = d_tiles.shape[0]
    hd = subln_g.shape[0] // 2
    vd = 2 * hd
    nt = s // t
    kernel = functools.partial(_attn_kernel, scale=hd ** -0.5, hd=hd)
    return pl.pallas_call(
        kernel,
        grid=(b, nh, nt, nt),
        in_specs=[pl.BlockSpec(memory_space=pltpu.SMEM),
                  pl.BlockSpec(memory_space=pltpu.SMEM),
                  pl.BlockSpec((None, t, vd), lambda bi, hi, qi, ki: (bi, qi, hi)),
                  pl.BlockSpec((None, t, vd), lambda bi, hi, qi, ki: (bi, jnp.minimum(ki, qi), nh + hi)),
                  pl.BlockSpec((None, t, vd), lambda bi, hi, qi, ki: (bi, jnp.minimum(ki, qi), 2 * nh + hi)),
                  pl.BlockSpec((None, 2, t, t), lambda bi, hi, qi, ki: (hi, 0, 0, 0)),
                  pl.BlockSpec((1, vd), lambda bi, hi, qi, ki: (0, 0))],
        out_specs=pl.BlockSpec((None, t, vd), lambda bi, hi, qi, ki: (bi, qi, hi)),
        out_shape=jax.ShapeDtypeStruct((b, s, nh * vd), BF16),
        scratch_shapes=[pltpu.VMEM((2, t, 1), F32), pltpu.VMEM((2, t, 1), F32),
                        pltpu.VMEM((2, t, vd), F32)],
        compiler_params=_params("parallel", "parallel", "parallel", "arbitrary"),
        name="attn_prompt",
    )(par, far, h3, h3, h3, d_tiles, subln_g.reshape(1, vd))


def _dec_attn_kernel(pt_ref, par_ref, qbd_ref, kc_ref, vc_ref, kn_ref, vn_ref, bias_ref, g_ref, o_ref,
                     m_scr, l_scr, acc_scr, *, scale, nh, vd, dec_seq):
    p = pl.program_id(1)
    last = pl.num_programs(1) - 1
    rows = 2 * dec_seq

    @pl.when(p == 0)
    def _init():
        m_scr[...] = jnp.full(m_scr.shape, -jnp.inf, F32)
        l_scr[...] = jnp.zeros(l_scr.shape, F32)
        acc_scr[...] = jnp.zeros(acc_scr.shape, F32)

    def step(k_f32, v_f32, bias):
        kb = k_f32.astype(BF16)
        vb = v_f32.astype(BF16)
        s = lax.dot_general(qbd_ref[...], kb, (((1,), (1,)), ((), ())), preferred_element_type=F32)
        s = s * scale + bias
        m_prev = m_scr[...]
        m_new = jnp.maximum(m_prev, jnp.max(s, axis=-1, keepdims=True))
        alpha = jnp.exp(m_prev - m_new)
        pr = jnp.exp(s - m_new)
        l_scr[...] = alpha * l_scr[...] + jnp.sum(pr, axis=-1, keepdims=True)
        m_scr[...] = m_new
        prb = pr.astype(BF16)
        for hh in range(nh):
            rs = slice(hh * rows, (hh + 1) * rows)
            acc_scr[rs, :] = alpha[rs] * acc_scr[rs, :] + jnp.dot(
                prb[rs], vb[:, hh * vd:(hh + 1) * vd], preferred_element_type=F32)

    @pl.when(p < last)
    def _far():
        step(kc_ref[...], vc_ref[...], bias_ref[0])

    @pl.when(p == last)
    def _tail():
        step(kc_ref[...], vc_ref[...], bias_ref[1])
        step(kn_ref[...], vn_ref[...], bias_ref[2])
        o_all = acc_scr[...] / l_scr[...]
        n_rows = o_all.shape[0]
        o = o_all - par_ref[0] * pltpu.roll(o_all, n_rows - dec_seq, 0)
        ms = jnp.mean(o * o, axis=-1, keepdims=True)
        o_ref[...] = (o * lax.rsqrt(ms + RMS_EPS) * g_ref[...]) * par_ref[1]


def attn_sample(page_table, par, qbd, cache_k, cache_v, layer, k_new, v_new, dec_bias, subln_g, dec_seq):
    db, n_pages = page_table.shape
    _, _, page, width = cache_k.shape
    nh = N_HEADS
    vd = width // nh
    n_rows = qbd.shape[1]
    kernel = functools.partial(_dec_attn_kernel, scale=(vd // 2) ** -0.5, nh=nh, vd=vd, dec_seq=dec_seq)
    grid_spec = pltpu.PrefetchScalarGridSpec(
        num_scalar_prefetch=1,
        grid=(db, n_pages),
        in_specs=[pl.BlockSpec(memory_space=pltpu.SMEM),
                  pl.BlockSpec((None, n_rows, width), lambda bi, pi, pt: (bi, 0, 0)),
                  pl.BlockSpec((None, None, page, width), lambda bi, pi, pt: (layer, pt[bi * n_pages + pi], 0, 0)),
                  pl.BlockSpec((None, None, page, width), lambda bi, pi, pt: (layer, pt[bi * n_pages + pi], 0, 0)),
                  pl.BlockSpec((None, page, width), lambda bi, pi, pt: (bi, 0, 0)),
                  pl.BlockSpec((None, page, width), lambda bi, pi, pt: (bi, 0, 0)),
                  pl.BlockSpec((3, n_rows, LANE), lambda bi, pi, pt: (0, 0, 0)),
                  pl.BlockSpec((1, vd), lambda bi, pi, pt: (0, 0))],
        out_specs=pl.BlockSpec((None, n_rows, vd), lambda bi, pi, pt: (bi, 0, 0)),
        scratch_shapes=[pltpu.VMEM((n_rows, 1), F32), pltpu.VMEM((n_rows, 1), F32),
                        pltpu.VMEM((n_rows, vd), F32)],
    )
    return pl.pallas_call(
        kernel,
        grid_spec=grid_spec,
        out_shape=jax.ShapeDtypeStruct((db, n_rows, vd), F32),
        compiler_params=_params("parallel", "arbitrary"),
        name="attn_sample",
    )(page_table.reshape(-1), par, qbd, cache_k, cache_v, k_new, v_new, dec_bias, subln_g.reshape(1, vd))


def _ssm_prep_kernel(lr_ref, li_ref, ldt_ref, lrw_ref, liw_ref, br_ref, bi_ref,
                     pw_re_ref, pw_im_ref, bb_re_ref, bb_im_ref):
    dt = jnp.exp(ldt_ref[...])
    zr = lr_ref[...] * dt
    zi = li_ref[...] * dt
    for kk in range(SUBLANE):
        mag = jnp.exp((kk + 1.0) * zr)
        pw_re_ref[kk] = mag * jnp.cos((kk + 1.0) * zi)
        pw_im_ref[kk] = mag * jnp.sin((kk + 1.0) * zi)
    lr = lrw_ref[...]
    li = liw_ref[...]
    mag = jnp.exp(lr * dt)
    x = mag * jnp.cos(li * dt) - 1.0
    y = mag * jnp.sin(li * dt)
    den = lr * lr + li * li
    cr = (x * lr + y * li) / den
    ci = (y * lr - x * li) / den
    br = br_ref[...]
    bi = bi_ref[...]
    bb_re_ref[...] = cr * br - ci * bi
    bb_im_ref[...] = cr * bi + ci * br


def ssm_prep(lam_re, lam_im, log_dt, b_re, b_im):
    g, p = lam_re.shape
    ch = b_re.shape[-1]
    wide = lambda a: jnp.repeat(a, ch, axis=-1)
    outs = pl.pallas_call(
        _ssm_prep_kernel,
        out_shape=[jax.ShapeDtypeStruct((SUBLANE, g, p), F32), jax.ShapeDtypeStruct((SUBLANE, g, p), F32),
                   jax.ShapeDtypeStruct((g, p * ch), F32), jax.ShapeDtypeStruct((g, p * ch), F32)],
        name="ssm_prep",
    )(lam_re, lam_im, log_dt.reshape(g, 1), wide(lam_re), wide(lam_im),
      b_re.reshape(g, p * ch), b_im.reshape(g, p * ch))
    pw_re, pw_im, bb_re, bb_im = outs
    return (pw_re.reshape(SUBLANE, g * p), pw_im.reshape(SUBLANE, g * p),
            bb_re.reshape(g, p, ch), bb_im.reshape(g, p, ch))


def _ssm_blockdiag(bb_re, bb_im, c_re, c_im):
    g, p, ch = bb_re.shape
    gl = LANE // ch
    nj = g // gl
    eye = jnp.eye(gl, dtype=bool)

    def in_proj(bb):
        x = bb.reshape(nj, gl, p, ch).transpose(0, 1, 3, 2)
        x = jnp.where(eye[None, :, None, :, None], x[:, :, :, None, :], 0.0)
        return x.reshape(nj, gl * ch, gl * p)

    def out_proj(c):
        x = c.reshape(nj, gl, ch, p).transpose(0, 1, 3, 2)
        x = jnp.where(eye[None, :, None, :, None], x[:, :, :, None, :], 0.0)
        return x.reshape(nj, gl * p, gl * ch)

    b_bd = jnp.concatenate([in_proj(bb_re), in_proj(bb_im)], axis=-1).astype(BF16)
    return b_bd, out_proj(c_re).astype(BF16), out_proj(c_im).astype(BF16)


def _ssm_kernel(u_ref, s0r_ref, s0i_ref, bbd_ref, cre_ref, cim_ref, pwr_ref, pwi_ref, d_ref, wg_ref,
                o_ref, sr_out_ref, si_out_ref, sre, sim, car_r, car_i, y_scr, *, tc, last_row, scan_w):
    ti = pl.program_id(1)
    nj, cin, two_w = bbd_ref.shape
    w = two_w // 2
    n_state = nj * w

    @pl.when(ti == 0)
    def _load_state():
        car_r[...] = s0r_ref[...]
        car_i[...] = s0i_ref[...]

    u = u_ref[...]
    ub = u.astype(BF16)
    for j in range(nj):
        r = jnp.dot(ub[:, j * cin:(j + 1) * cin], bbd_ref[j], preferred_element_type=F32)
        sre[:, j * w:(j + 1) * w] = r[:, :w]
        sim[:, j * w:(j + 1) * w] = r[:, w:]

    row = lax.broadcasted_iota(jnp.int32, (SUBLANE, scan_w), 0)
    for c0 in range(0, n_state, scan_w):
        cs = slice(c0, c0 + scan_w)
        a_re = [jnp.broadcast_to(pwr_ref[kk:kk + 1, cs], (SUBLANE, scan_w)) for kk in (0, 1, 3)]
        a_im = [jnp.broadcast_to(pwi_ref[kk:kk + 1, cs], (SUBLANE, scan_w)) for kk in (0, 1, 3)]
        a8_re = pwr_ref[:, cs]
        a8_im = pwi_ref[:, cs]

        def body(i, carry):
            c_re, c_im = carry
            r0 = pl.multiple_of(i * SUBLANE, SUBLANE)
            x_re = sre[pl.ds(r0, SUBLANE), cs]
            x_im = sim[pl.ds(r0, SUBLANE), cs]
            for step, shift in enumerate((1, 2, 4)):
                keep = row >= shift
                s_re = jnp.where(keep, pltpu.roll(x_re, shift, 0), 0.0)
                s_im = jnp.where(keep, pltpu.roll(x_im, shift, 0), 0.0)
                x_re, x_im = (x_re + (a_re[step] * s_re - a_im[step] * s_im),
                              x_im + (a_re[step] * s_im + a_im[step] * s_re))
            x_re, x_im = (x_re + (a8_re * c_re - a8_im * c_im),
                          x_im + (a8_re * c_im + a8_im * c_re))
            sre[pl.ds(r0, SUBLANE), cs] = x_re
            sim[pl.ds(r0, SUBLANE), cs] = x_im
            return (jnp.broadcast_to(x_re[SUBLANE - 1:SUBLANE], (SUBLANE, scan_w)),
                    jnp.broadcast_to(x_im[SUBLANE - 1:SUBLANE], (SUBLANE, scan_w)))

        init = (jnp.broadcast_to(car_r[:, cs], (SUBLANE, scan_w)),
                jnp.broadcast_to(car_i[:, cs], (SUBLANE, scan_w)))
        lax.fori_loop(0, tc // SUBLANE, body, init)

    car_r[...] = sre[tc - 1:tc, :]
    car_i[...] = sim[tc - 1:tc, :]

    @pl.when(ti == pl.num_programs(1) - 1)
    def _final_state():
        sr_out_ref[...] = sre[last_row:last_row + 1, :]
        si_out_ref[...] = sim[last_row:last_row + 1, :]

    for j in range(nj):
        y_scr[:, j * cin:(j + 1) * cin] = (
            jnp.dot(sre[:, j * w:(j + 1) * w].astype(BF16), cre_ref[j], preferred_element_type=F32)
            - jnp.dot(sim[:, j * w:(j + 1) * w].astype(BF16), cim_ref[j], preferred_element_type=F32))
    y = y_scr[...] + d_ref[...] * u
    gg = jax.nn.gelu(y)
    gate = jnp.dot(gg.astype(BF16), wg_ref[...], preferred_element_type=F32)
    o_ref[...] = (gg * jax.nn.sigmoid(gate)).astype(o_ref.dtype)


def ssm_mix(h3, col_block, seq_len, s0_re, s0_im, prep, d_skip, w_glu):
    b_bd, c_re_bd, c_im_bd, pw_re, pw_im = prep
    b, l, _ = h3.shape
    nj, cin, two_w = b_bd.shape
    width = nj * cin
    n_state = nj * two_w // 2
    tc = _tile(l, 256, SUBLANE)
    nt = l // tc
    last_row = (seq_len - 1) % tc
    kernel = functools.partial(_ssm_kernel, tc=tc, last_row=last_row, scan_w=256)
    const3 = lambda bi, ti: (0, 0, 0)
    const2 = lambda bi, ti: (0, 0)
    out, s_re, s_im = pl.pallas_call(
        kernel,
        grid=(b, nt),
        in_specs=[pl.BlockSpec((None, tc, width), lambda bi, ti: (bi, ti, col_block)),
                  pl.BlockSpec((None, 1, n_state), lambda bi, ti: (bi, 0, 0)),
                  pl.BlockSpec((None, 1, n_state), lambda bi, ti: (bi, 0, 0)),
                  pl.BlockSpec(b_bd.shape, const3),
                  pl.BlockSpec(c_re_bd.shape, const3),
                  pl.BlockSpec(c_im_bd.shape, const3),
                  pl.BlockSpec(pw_re.shape, const2),
                  pl.BlockSpec(pw_im.shape, const2),
                  pl.BlockSpec((1, width), const2),
                  pl.BlockSpec(w_glu.shape, const2)],
        out_specs=[pl.BlockSpec((None, tc, width), lambda bi, ti: (bi, ti, 0)),
                   pl.BlockSpec((None, 1, n_state), lambda bi, ti: (bi, 0, 0)),
                   pl.BlockSpec((None, 1, n_state), lambda bi, ti: (bi, 0, 0))],
        out_shape=[jax.ShapeDtypeStruct((b, l, width), BF16),
                   jax.ShapeDtypeStruct((b, 1, n_state), F32),
                   jax.ShapeDtypeStruct((b, 1, n_state), F32)],
        scratch_shapes=[pltpu.VMEM((tc, n_state), F32), pltpu.VMEM((tc, n_state), F32),
                        pltpu.VMEM((1, n_state), F32), pltpu.VMEM((1, n_state), F32),
                        pltpu.VMEM((tc, width), F32)],
        compiler_params=_params("parallel", "arbitrary"),
        name="ssm_mix",
    )(h3, s0_re.reshape(b, 1, n_state), s0_im.reshape(b, 1, n_state), b_bd, c_re_bd, c_im_bd,
      pw_re, pw_im, d_skip.reshape(1, width), w_glu)
    return out, s_re.reshape(b, n_state), s_im.reshape(b, n_state)


def _pool_kernel(u_ref, pre_ref, w_ref, sc_ref, o_ref, xp, *, tc, n_prev):
    ti = pl.program_id(1)
    n_win, grp, _ = w_ref.shape

    @pl.when(ti == 0)
    def _prefix():
        xp[0:POOL_HIST, :] = pre_ref[...]

    @pl.when(ti > 0)
    def _history():
        xp[0:POOL_HIST, :] = xp[tc:tc + POOL_HIST, :]

    u = u_ref[...]
    xp[POOL_HIST:POOL_HIST + tc, :] = u
    t_idx = ti * tc + lax.broadcasted_iota(jnp.int32, (tc, 1), 0) + 1
    for gi, win in enumerate(POOL_WINDOWS[:n_win]):
        cs = slice(gi * grp, (gi + 1) * grp)
        acc = u[:, cs]
        for dlt in range(1, win):
            acc = acc + xp[POOL_HIST - dlt:POOL_HIST - dlt + tc, cs]
        count = jnp.minimum(n_prev + t_idx, win).astype(F32)
        m = acc / count - u[:, cs]
        y = jnp.dot(m.astype(BF16), w_ref[gi], preferred_element_type=F32)
        o_ref[:, cs] = (y * sc_ref[:, cs]).astype(o_ref.dtype)


def pool_mix(h3, col_block, prefix, n_prev, pool_w, pool_scale):
    b, l, _ = h3.shape
    width = pool_scale.shape[0]
    tc = _tile(l, 256, SUBLANE)
    kernel = functools.partial(_pool_kernel, tc=tc, n_prev=n_prev)
    return pl.pallas_call(
        kernel,
        grid=(b, l // tc),
        in_specs=[pl.BlockSpec((None, tc, width), lambda bi, ti: (bi, ti, col_block)),
                  pl.BlockSpec((None, POOL_HIST, width), lambda bi, ti: (bi, 0, 0)),
                  pl.BlockSpec(pool_w.shape, lambda bi, ti: (0, 0, 0)),
                  pl.BlockSpec((1, width), lambda bi, ti: (0, 0))],
        out_specs=pl.BlockSpec((None, tc, width), lambda bi, ti: (bi, ti, 0)),
        out_shape=jax.ShapeDtypeStruct((b, l, width), BF16),
        scratch_shapes=[pltpu.VMEM((POOL_HIST + tc, width), F32)],
        compiler_params=_params("parallel", "arbitrary"),
        name="pool_mix",
    )(h3, prefix, pool_w, pool_scale.reshape(1, width))


def _ffn(x, g_pre, g_post, wg, wu, wd):
    act = gateup(rmsnorm_cast(x, g_pre), wg, wu)
    return resnorm(x, down(act, wd), g_post, 0.5)


def kernel(x_prompt, x_sample, cache_k, cache_v, state_ssm_re, state_ssm_im, state_pool, page_table,
           norm_g, w_ffn_gate, w_ffn_up, w_ffn_down, w_in, w_out, rel_bias, diff_lambda, diff_subln,
           ssm_lam_re, ssm_lam_im, ssm_log_dt, ssm_b_re, ssm_b_im, ssm_c_re, ssm_c_im, ssm_d, ssm_w_glu,
           pool_w, pool_scale):
    bp, seq, d_model = x_prompt.shape
    db, dec_seq, _ = x_sample.shape
    depth = norm_g.shape[0]
    _, n_pool, page, nh, _, hd = cache_k.shape
    vd = 2 * hd
    attn_w = nh * vd
    ssm_w = ssm_d.shape[1]
    pool_wd = pool_scale.shape[1]
    n_groups, n_state_g = ssm_lam_re.shape[1:]
    n_state = n_groups * n_state_g
    n_buf = state_pool.shape[2]
    past_len = page_table.shape[1] * page
    assert attn_w % vd == 0 and ssm_w == vd * (ssm_w // vd) and pool_wd == ssm_w
    ssm_blk = 3 * attn_w // ssm_w
    pool_blk = (3 * attn_w + ssm_w) // pool_wd
    dec_pad = _tile(page, page, SUBLANE)
    t_attn = _tile(seq, 512, LANE)

    lam_init = [0.8 - 0.6 * math.exp(-0.3 * l) for l in range(depth)]
    lams = diff_lambdas(diff_lambda, lam_init)
    d_tiles, dec_tab = bias_tables(rel_bias, t_attn, dec_seq)
    far = dec_tab[:, 0, 0, 0]
    dec_bias = dec_tab.transpose(1, 0, 2, 3).reshape(3, nh * SUBLANE, LANE)
    cache_k4 = cache_k.reshape(depth, n_pool, page, attn_w)
    cache_v4 = cache_v.reshape(depth, n_pool, page, attn_w)
    eye_hc = jnp.eye(2 * nh, dtype=bool)

    xp = x_prompt.reshape(bp * seq, d_model)
    xs = x_sample.reshape(db * dec_seq, d_model)
    outs = [[] for _ in range(10)]
    for l in range(depth):
        g = norm_g[l]
        par = jnp.stack([lams[l], jnp.asarray(1.0 - lam_init[l], F32)])
        pw_re, pw_im, bb_re, bb_im = ssm_prep(ssm_lam_re[l], ssm_lam_im[l], ssm_log_dt[l], ssm_b_re[l], ssm_b_im[l])
        b_bd, c_re_bd, c_im_bd = _ssm_blockdiag(bb_re, bb_im, ssm_c_re[l], ssm_c_im[l])
        prep = (b_bd, c_re_bd, c_im_bd, pw_re, pw_im)
        w_glu = ssm_w_glu[l].astype(BF16)
        pw = pool_w[l].astype(BF16)

        xp = _ffn(xp, g[0], g[1], w_ffn_gate[l, 0], w_ffn_up[l, 0], w_ffn_down[l, 0])
        h = matmul(rmsnorm_cast(xp, g[2]), w_in[l])
        h3 = h.reshape(bp, seq, -1)
        a_out = attn_prompt(h3, par, far, d_tiles, diff_subln[l], t_attn)
        zeros_state = jnp.zeros((bp, n_state), F32)
        s_out, s_re, s_im = ssm_mix(h3, ssm_blk, seq, zeros_state, zeros_state, prep, ssm_d[l], w_glu)
        p_out = pool_mix(h3, pool_blk, jnp.zeros((bp, POOL_HIST, pool_wd), F32), 0, pw, pool_scale[l])
        mixed = jnp.concatenate([a_out, s_out, p_out], axis=-1).reshape(bp * seq, -1)
        xp = resnorm(xp, matmul(mixed, w_out[l]), g[3], 1.0)
        xp = _ffn(xp, g[4], g[5], w_ffn_gate[l, 1], w_ffn_up[l, 1], w_ffn_down[l, 1])
        outs[0].append(h3[..., attn_w:2 * attn_w].reshape(bp, seq, nh, 2, hd))
        outs[1].append(h3[..., 2 * attn_w:3 * attn_w].reshape(bp, seq, nh, vd))
        outs[2].append(s_re.reshape(bp, n_groups, n_state_g))
        outs[3].append(s_im.reshape(bp, n_groups, n_state_g))
        outs[4].append(h3[:, seq - n_buf:, 3 * attn_w + ssm_w:])

        xs = _ffn(xs, g[0], g[1], w_ffn_gate[l, 0], w_ffn_up[l, 0], w_ffn_down[l, 0])
        hs = matmul(rmsnorm_cast(xs, g[2]), w_in[l]).reshape(db, dec_seq, -1)
        q = hs[..., :attn_w].reshape(db, dec_seq, 2 * nh, hd).transpose(0, 2, 1, 3)
        qbd = jnp.where(eye_hc[None, :, None, :, None], q[:, :, :, None, :], 0.0)
        qbd = qbd.reshape(db, 2 * nh * dec_seq, attn_w).astype(BF16)
        k_new = hs[..., attn_w:2 * attn_w]
        v_new = hs[..., 2 * attn_w:3 * attn_w]
        pad_tok = ((0, 0), (0, dec_pad - dec_seq), (0, 0))
        a_rows = attn_sample(page_table, par, qbd, cache_k4, cache_v4, l, jnp.pad(k_new, pad_tok),
                             jnp.pad(v_new, pad_tok), dec_bias, diff_subln[l], dec_seq)
        a_s = a_rows.reshape(db, nh, 2, dec_seq, vd)[:, :, 0].transpose(0, 2, 1, 3).reshape(db, dec_seq, attn_w)
        l_pad = -(-dec_seq // SUBLANE) * SUBLANE
        hs_pad = jnp.pad(hs, ((0, 0), (0, l_pad - dec_seq), (0, 0)))
        s_s, ss_re, ss_im = ssm_mix(hs_pad, ssm_blk, dec_seq, state_ssm_re[l].reshape(db, n_state),
                                    state_ssm_im[l].reshape(db, n_state), prep, ssm_d[l], w_glu)
        prefix = jnp.pad(state_pool[l], ((0, 0), (POOL_HIST - n_buf, 0), (0, 0)))
        p_s = pool_mix(hs_pad, pool_blk, prefix, past_len, pw, pool_scale[l])
        mixed_s = jnp.concatenate([a_s.astype(BF16), s_s[:, :dec_seq], p_s[:, :dec_seq]], axis=-1)
        xs = resnorm(xs, matmul(mixed_s.reshape(db * dec_seq, -1), w_out[l]), g[3], 1.0)
        xs = _ffn(xs, g[4], g[5], w_ffn_gate[l, 1], w_ffn_up[l, 1], w_ffn_down[l, 1])
        up_s = hs[..., 3 * attn_w + ssm_w:]
        outs[5].append(k_new.reshape(db, dec_seq, nh, 2, hd))
        outs[6].append(v_new.reshape(db, dec_seq, nh, vd))
        outs[7].append(ss_re.reshape(db, n_groups, n_state_g))
        outs[8].append(ss_im.reshape(db, n_groups, n_state_g))
        outs[9].append(jnp.concatenate([state_pool[l], up_s], axis=1)[:, -n_buf:])
    return (xp.reshape(bp, seq, d_model), xs.reshape(db, dec_seq, d_model),
            *[jnp.stack(o) for o in outs])
```

```python
import functools
import math

import jax
import jax.numpy as jnp
from jax import lax
from jax.experimental import pallas as pl
from jax.experimental.pallas import tpu as pltpu

F32 = jnp.float32
BF16 = jnp.bfloat16

RMS_EPS = 1e-6
N_HEADS = 8
NUM_BUCKETS = 32
MAX_DISTANCE = 128
POOL_WINDOWS = (2, 4, 8, 16)
POOL_HIST = 16
SSM_CH = 16
LANE = 128
SUBLANE = 8
VMEM_LIMIT = 56 * 1024 * 1024
ROW_TILE = 2048


def _tile(dim, pref, align):
    t = min(pref, dim)
    t -= t % align
    while t >= align:
        if dim % t == 0:
            return t
        t -= align
    return dim


def _params(*sem):
    return pltpu.CompilerParams(dimension_semantics=sem, vmem_limit_bytes=VMEM_LIMIT)


def _rmsnorm_kernel(x_ref, g_ref, o_ref):
    x = x_ref[...]
    ms = jnp.mean(x * x, axis=-1, keepdims=True)
    o_ref[...] = (x * lax.rsqrt(ms + RMS_EPS) * g_ref[...]).astype(o_ref.dtype)


def rmsnorm_cast(x, g):
    m, d = x.shape
    tm = _tile(m, 256, SUBLANE)
    return pl.pallas_call(
        _rmsnorm_kernel,
        grid=(m // tm,),
        in_specs=[pl.BlockSpec((tm, d), lambda i: (i, 0)),
                  pl.BlockSpec((1, d), lambda i: (0, 0))],
        out_specs=pl.BlockSpec((tm, d), lambda i: (i, 0)),
        out_shape=jax.ShapeDtypeStruct((m, d), BF16),
        compiler_params=_params("parallel"),
        name="rmsnorm_cast",
    )(x, g.reshape(1, d))


def _resnorm_kernel(x_ref, y_ref, g_ref, gn_ref, o_ref, on_ref, *, scale):
    y = y_ref[...]
    ms = jnp.mean(y * y, axis=-1, keepdims=True)
    x = x_ref[...] + scale * (y * lax.rsqrt(ms + RMS_EPS) * g_ref[...])
    o_ref[...] = x
    ms = jnp.mean(x * x, axis=-1, keepdims=True)
    on_ref[...] = (x * lax.rsqrt(ms + RMS_EPS) * gn_ref[...]).astype(on_ref.dtype)


def resnorm(x, y, g, scale, g_next):
    m, d = x.shape
    tm = _tile(m, 256, 16)
    row = pl.BlockSpec((tm, d), lambda i: (i, 0))
    vec = pl.BlockSpec((1, d), lambda i: (0, 0))
    return pl.pallas_call(
        functools.partial(_resnorm_kernel, scale=scale),
        grid=(m // tm,),
        in_specs=[row, row, vec, vec],
        out_specs=[row, row],
        out_shape=[jax.ShapeDtypeStruct((m, d), F32), jax.ShapeDtypeStruct((m, d), BF16)],
        compiler_params=_params("parallel"),
        name="resnorm",
    )(x, y, g.reshape(1, d), g_next.reshape(1, d))


def _gateup_kernel(a_ref, wg_ref, wu_ref, o_ref):
    a = a_ref[...]
    g = jnp.dot(a, wg_ref[...].astype(BF16), preferred_element_type=F32)
    u = jnp.dot(a, wu_ref[...].astype(BF16), preferred_element_type=F32)
    o_ref[...] = (g * jax.nn.sigmoid(g) * u).astype(o_ref.dtype)


def gateup(a, wg, wu):
    m, k = a.shape
    n = wg.shape[1]
    tm = _tile(m, ROW_TILE, 16)
    tn = _tile(n, 256, LANE)
    return pl.pallas_call(
        _gateup_kernel,
        grid=(m // tm, n // tn),
        in_specs=[pl.BlockSpec((tm, k), lambda i, j: (i, 0), pipeline_mode=pl.Buffered(1)),
                  pl.BlockSpec((k, tn), lambda i, j: (0, j)),
                  pl.BlockSpec((k, tn), lambda i, j: (0, j))],
        out_specs=pl.BlockSpec((None, tm, tn), lambda i, j: (j, i, 0)),
        out_shape=jax.ShapeDtypeStruct((n // tn, m, tn), BF16),
        compiler_params=_params("parallel", "arbitrary"),
        name="gateup",
    )(a, wg, wu)


def _down_kernel(a_ref, w_ref, o_ref, *, n_chunk):
    k = pl.program_id(1)
    a = a_ref[...]
    n = o_ref.shape[1]

    @pl.when(k == 0)
    def _first():
        for c in range(0, n, n_chunk):
            o_ref[:, c:c + n_chunk] = jnp.dot(a, w_ref[:, c:c + n_chunk].astype(BF16),
                                              preferred_element_type=F32)

    @pl.when(k > 0)
    def _rest():
        for c in range(0, n, n_chunk):
            o_ref[:, c:c + n_chunk] += jnp.dot(a, w_ref[:, c:c + n_chunk].astype(BF16),
                                               preferred_element_type=F32)


def down(act, w):
    nk, m, tk = act.shape
    n = w.shape[1]
    tm = _tile(m, ROW_TILE, 16)
    n_chunk = _tile(n, 512, LANE)
    return pl.pallas_call(
        functools.partial(_down_kernel, n_chunk=n_chunk),
        grid=(m // tm, nk),
        in_specs=[pl.BlockSpec((None, tm, tk), lambda i, kk: (kk, i, 0)),
                  pl.BlockSpec((tk, n), lambda i, kk: (kk, 0))],
        out_specs=pl.BlockSpec((tm, n), lambda i, kk: (i, 0), pipeline_mode=pl.Buffered(1)),
        out_shape=jax.ShapeDtypeStruct((m, n), F32),
        compiler_params=_params("parallel", "arbitrary"),
        name="down",
    )(act, w)


def _mm_kernel(a_ref, w_ref, o_ref):
    o_ref[...] = jnp.dot(a_ref[...], w_ref[...].astype(BF16), preferred_element_type=F32)


def matmul(a, w):
    m, k = a.shape
    n = w.shape[1]
    tm = _tile(m, ROW_TILE, 16)
    tn = _tile(n, 512, LANE)
    return pl.pallas_call(
        _mm_kernel,
        grid=(m // tm, n // tn),
        in_specs=[pl.BlockSpec((tm, k), lambda i, j: (i, 0), pipeline_mode=pl.Buffered(1)),
                  pl.BlockSpec((k, tn), lambda i, j: (0, j))],
        out_specs=pl.BlockSpec((tm, tn), lambda i, j: (i, j)),
        out_shape=jax.ShapeDtypeStruct((m, n), F32),
        compiler_params=_params("parallel", "arbitrary"),
        name="matmul",
    )(a, w)


def _t5_bias(rel, rb_ref, h):
    n = jnp.maximum(rel, 0)
    max_exact = NUM_BUCKETS // 2
    nf = jnp.maximum(n, 1).astype(F32)
    large = max_exact + (jnp.log(nf / max_exact) / math.log(MAX_DISTANCE / max_exact)
                         * (NUM_BUCKETS - max_exact)).astype(jnp.int32)
    large = jnp.minimum(large, NUM_BUCKETS - 1)
    bucket = jnp.where(n < max_exact, n, large)
    out = jnp.zeros(rel.shape, F32)
    for b in range(NUM_BUCKETS):
        out = jnp.where(bucket == b, rb_ref[b, h], out)
    return out


def _bias_kernel(rb_ref, d_ref, dec_ref, *, t, dec_seq):
    h = pl.program_id(0)
    r = lax.broadcasted_iota(jnp.int32, (t, t), 0)
    c = lax.broadcasted_iota(jnp.int32, (t, t), 1)
    d_ref[0] = jnp.where(r >= c, _t5_bias(r - c, rb_ref, h), -jnp.inf)
    d_ref[1] = _t5_bias(t + r - c, rb_ref, h)
    row = lax.broadcasted_iota(jnp.int32, (SUBLANE, LANE), 0)
    col = lax.broadcasted_iota(jnp.int32, (SUBLANE, LANE), 1)
    tq = row % dec_seq
    dec_ref[0] = _t5_bias(jnp.full((SUBLANE, LANE), 2 * MAX_DISTANCE, jnp.int32), rb_ref, h)
    dec_ref[1] = _t5_bias(LANE + tq - col, rb_ref, h)
    dec_ref[2] = jnp.where((col <= tq) & (col < dec_seq), _t5_bias(tq - col, rb_ref, h), -jnp.inf)


def bias_tables(rel_bias, t, dec_seq):
    nb, nh = rel_bias.shape
    return pl.pallas_call(
        functools.partial(_bias_kernel, t=t, dec_seq=dec_seq),
        grid=(nh,),
        in_specs=[pl.BlockSpec(memory_space=pltpu.SMEM)],
        out_specs=[pl.BlockSpec((None, 2, t, t), lambda h: (h, 0, 0, 0)),
                   pl.BlockSpec((None, 3, SUBLANE, LANE), lambda h: (h, 0, 0, 0))],
        out_shape=[jax.ShapeDtypeStruct((nh, 2, t, t), F32),
                   jax.ShapeDtypeStruct((nh, 3, SUBLANE, LANE), F32)],
        compiler_params=_params("arbitrary"),
        name="bias_tables",
    )(rel_bias)


def _lambda_kernel(lp_ref, init_ref, o_ref):
    lp = lp_ref[...]
    s1 = jnp.sum(lp[:, 0, :] * lp[:, 1, :], axis=-1, keepdims=True)
    s2 = jnp.sum(lp[:, 2, :] * lp[:, 3, :], axis=-1, keepdims=True)
    o_ref[...] = jnp.broadcast_to(jnp.exp(s1) - jnp.exp(s2) + init_ref[...], o_ref.shape)


def diff_lambdas(diff_lambda, lam_init):
    depth = diff_lambda.shape[0]
    out = pl.pallas_call(
        _lambda_kernel,
        out_shape=jax.ShapeDtypeStruct((depth, LANE), F32),
        name="diff_lambdas",
    )(diff_lambda, jnp.asarray(lam_init, F32).reshape(depth, 1))
    return out[:, 0]


def _attn_kernel(par_ref, far_ref, q_ref, k_ref, v_ref, d_ref, g_ref, o_ref,
                 m_scr, l_scr, acc_scr, *, scale, hd):
    h = pl.program_id(1)
    qi = pl.program_id(2)
    ki = pl.program_id(3)

    @pl.when(ki == 0)
    def _init():
        m_scr[...] = jnp.full(m_scr.shape, -jnp.inf, F32)
        l_scr[...] = jnp.zeros(l_scr.shape, F32)
        acc_scr[...] = jnp.zeros(acc_scr.shape, F32)

    def step(bias):
        q = q_ref[...].astype(BF16)
        k = k_ref[...].astype(BF16)
        v = v_ref[...].astype(BF16)
        for c in range(2):
            s = lax.dot_general(q[:, c * hd:(c + 1) * hd], k[:, c * hd:(c + 1) * hd],
                                (((1,), (1,)), ((), ())), preferred_element_type=F32)
            s = s * scale + bias
            m_prev = m_scr[c]
            m_new = jnp.maximum(m_prev, jnp.max(s, axis=-1, keepdims=True))
            alpha = jnp.exp(m_prev - m_new)
            p = jnp.exp(s - m_new)
            l_scr[c] = alpha * l_scr[c] + jnp.sum(p, axis=-1, keepdims=True)
            acc_scr[c] = alpha * acc_scr[c] + jnp.dot(p.astype(BF16), v, preferred_element_type=F32)
            m_scr[c] = m_new

    @pl.when(ki == qi)
    def _diag():
        step(d_ref[0])

    @pl.when(ki == qi - 1)
    def _sub():
        step(d_ref[1])

    @pl.when(ki < qi - 1)
    def _far():
        step(far_ref[h])

    @pl.when(ki == qi)
    def _finish():
        o = acc_scr[0] / l_scr[0] - par_ref[0] * (acc_scr[1] / l_scr[1])
        ms = jnp.mean(o * o, axis=-1, keepdims=True)
        o_ref[...] = ((o * lax.rsqrt(ms + RMS_EPS) * g_ref[...]) * par_ref[1]).astype(o_ref.dtype)


def attn_prompt(h3, par, far, d_tiles, subln_g, t):
    b, s, _ = h3.shape
    nh = d_tiles.shape[0]
    hd = subln_g.shape[0] // 2
    vd = 2 * hd
    nt = s // t
    kernel = functools.partial(_attn_kernel, scale=hd ** -0.5, hd=hd)
    return pl.pallas_call(
        kernel,
        grid=(b, nh, nt, nt),
        in_specs=[pl.BlockSpec(memory_space=pltpu.SMEM),
                  pl.BlockSpec(memory_space=pltpu.SMEM),
                  pl.BlockSpec((None, t, vd), lambda bi, hi, qi, ki: (bi, qi, hi)),
                  pl.BlockSpec((None, t, vd), lambda bi, hi, qi, ki: (bi, jnp.minimum(ki, qi), nh + hi)),
                  pl.BlockSpec((None, t, vd), lambda bi, hi, qi, ki: (bi, jnp.minimum(ki, qi), 2 * nh + hi)),
                  pl.BlockSpec((None, 2, t, t), lambda bi, hi, qi, ki: (hi, 0, 0, 0)),
                  pl.BlockSpec((1, vd), lambda bi, hi, qi, ki: (0, 0))],
        out_specs=pl.BlockSpec((None, t, vd), lambda bi, hi, qi, ki: (bi, qi, hi)),
        out_shape=jax.ShapeDtypeStruct((b, s, nh * vd), BF16),
        scratch_shapes=[pltpu.VMEM((2, t, 1), F32), pltpu.VMEM((2, t, 1), F32),
                        pltpu.VMEM((2, t, vd), F32)],
        compiler_params=_params("parallel", "parallel", "parallel", "arbitrary"),
        name="attn_prompt",
    )(par, far, h3, h3, h3, d_tiles, subln_g.reshape(1, vd))


def _dec_attn_kernel(pt_ref, par_ref, qbd_ref, kc_ref, vc_ref, kn_ref, vn_ref, bias_ref, g_ref, o_ref,
                     m_scr, l_scr, acc_scr, kb_scr, vb_scr, *, scale, nh, vd, hd, page, dec_seq):
    p = pl.program_id(1)
    last = pl.num_programs(1) - 1
    rows = 2 * dec_seq

    @pl.when(p == 0)
    def _init():
        m_scr[...] = jnp.full(m_scr.shape, -jnp.inf, F32)
        l_scr[...] = jnp.zeros(l_scr.shape, F32)
        acc_scr[...] = jnp.zeros(acc_scr.shape, F32)

    def step(kb, vb, bias):
        s = lax.dot_general(qbd_ref[...], kb, (((1,), (1,)), ((), ())), preferred_element_type=F32)
        s = s * scale + bias
        m_prev = m_scr[...]
        m_new = jnp.maximum(m_prev, jnp.max(s, axis=-1, keepdims=True))
        alpha = jnp.exp(m_prev - m_new)
        pr = jnp.exp(s - m_new)
        l_scr[...] = alpha * l_scr[...] + jnp.sum(pr, axis=-1, keepdims=True)
        m_scr[...] = m_new
        prb = pr.astype(BF16)
        for hh in range(nh):
            rs = slice(hh * rows, (hh + 1) * rows)
            acc_scr[rs, :] = alpha[rs] * acc_scr[rs, :] + jnp.dot(
                prb[rs], vb[:, hh * vd:(hh + 1) * vd], preferred_element_type=F32)

    def cached_step(bias):
        slabs = kc_ref.shape[0] // page
        for j in range(slabs):
            kb_scr[:, j * hd:(j + 1) * hd] = kc_ref[pl.ds(j, page, stride=slabs), :].astype(BF16)
            vb_scr[:, j * hd:(j + 1) * hd] = vc_ref[pl.ds(j, page, stride=slabs), :].astype(BF16)
        step(kb_scr[...], vb_scr[...], bias)

    @pl.when(p < last)
    def _far():
        cached_step(bias_ref[0])

    @pl.when(p == last)
    def _tail():
        cached_step(bias_ref[1])
        step(kn_ref[...].astype(BF16), vn_ref[...].astype(BF16), bias_ref[2])
        o_all = acc_scr[...] / l_scr[...]
        n_rows = o_all.shape[0]
        o = o_all - par_ref[0] * pltpu.roll(o_all, n_rows - dec_seq, 0)
        ms = jnp.mean(o * o, axis=-1, keepdims=True)
        o_ref[...] = (o * lax.rsqrt(ms + RMS_EPS) * g_ref[...]) * par_ref[1]


def attn_sample(page_table, par, qbd, cache_k, cache_v, layer, k_new, v_new, dec_bias, subln_g, dec_seq):
    db, n_pages = page_table.shape
    _, _, page_rows, hd = cache_k.shape
    nh = N_HEADS
    vd = 2 * hd
    width = nh * vd
    page = page_rows // (2 * nh)
    n_rows = qbd.shape[1]
    kernel = functools.partial(_dec_attn_kernel, scale=hd ** -0.5, nh=nh, vd=vd, hd=hd, page=page,
                               dec_seq=dec_seq)
    grid_spec = pltpu.PrefetchScalarGridSpec(
        num_scalar_prefetch=1,
        grid=(db, n_pages),
        in_specs=[pl.BlockSpec(memory_space=pltpu.SMEM),
                  pl.BlockSpec((None, n_rows, width), lambda bi, pi, pt: (bi, 0, 0)),
                  pl.BlockSpec((None, None, page_rows, hd),
                               lambda bi, pi, pt: (layer, pt[bi * n_pages + pi], 0, 0)),
                  pl.BlockSpec((None, None, page_rows, hd),
                               lambda bi, pi, pt: (layer, pt[bi * n_pages + pi], 0, 0)),
                  pl.BlockSpec((None, page, width), lambda bi, pi, pt: (bi, 0, 0)),
                  pl.BlockSpec((None, page, width), lambda bi, pi, pt: (bi, 0, 0)),
                  pl.BlockSpec((3, n_rows, LANE), lambda bi, pi, pt: (0, 0, 0)),
                  pl.BlockSpec((1, vd), lambda bi, pi, pt: (0, 0))],
        out_specs=pl.BlockSpec((None, n_rows, vd), lambda bi, pi, pt: (bi, 0, 0)),
        scratch_shapes=[pltpu.VMEM((n_rows, 1), F32), pltpu.VMEM((n_rows, 1), F32),
                        pltpu.VMEM((n_rows, vd), F32),
                        pltpu.VMEM((page, width), BF16), pltpu.VMEM((page, width), BF16)],
    )
    return pl.pallas_call(
        kernel,
        grid_spec=grid_spec,
        out_shape=jax.ShapeDtypeStruct((db, n_rows, vd), F32),
        compiler_params=_params("parallel", "arbitrary"),
        name="attn_sample",
    )(page_table.reshape(-1), par, qbd, cache_k, cache_v, k_new, v_new, dec_bias, subln_g.reshape(1, vd))


def _ssm_prep_kernel(lr_ref, li_ref, ldt_ref, lrw_ref, liw_ref, br_ref, bi_ref,
                     pw_re_ref, pw_im_ref, bb_re_ref, bb_im_ref):
    dt = jnp.exp(ldt_ref[...])
    zr = lr_ref[...] * dt
    zi = li_ref[...] * dt
    for kk in range(SUBLANE):
        mag = jnp.exp((kk + 1.0) * zr)
        pw_re_ref[kk] = mag * jnp.cos((kk + 1.0) * zi)
        pw_im_ref[kk] = mag * jnp.sin((kk + 1.0) * zi)
    lr = lrw_ref[...]
    li = liw_ref[...]
    mag = jnp.exp(lr * dt)
    x = mag * jnp.cos(li * dt) - 1.0
    y = mag * jnp.sin(li * dt)
    den = lr * lr + li * li
    cr = (x * lr + y * li) / den
    ci = (y * lr - x * li) / den
    br = br_ref[...]
    bi = bi_ref[...]
    bb_re_ref[...] = cr * br - ci * bi
    bb_im_ref[...] = cr * bi + ci * br


def ssm_prep(lam_re, lam_im, log_dt, b_re, b_im):
    g, p = lam_re.shape
    ch = b_re.shape[-1]
    wide = lambda a: jnp.repeat(a, ch, axis=-1)
    outs = pl.pallas_call(
        _ssm_prep_kernel,
        out_shape=[jax.ShapeDtypeStruct((SUBLANE, g, p), F32), jax.ShapeDtypeStruct((SUBLANE, g, p), F32),
                   jax.ShapeDtypeStruct((g, p * ch), F32), jax.ShapeDtypeStruct((g, p * ch), F32)],
        name="ssm_prep",
    )(lam_re, lam_im, log_dt.reshape(g, 1), wide(lam_re), wide(lam_im),
      b_re.reshape(g, p * ch), b_im.reshape(g, p * ch))
    pw_re, pw_im, bb_re, bb_im = outs
    return (pw_re.reshape(SUBLANE, g * p), pw_im.reshape(SUBLANE, g * p),
            bb_re.reshape(g, p, ch), bb_im.reshape(g, p, ch))


def _ssm_blockdiag(bb_re, bb_im, c_re, c_im):
    g, p, ch = bb_re.shape
    gl = LANE // ch
    nj = g // gl
    eye = jnp.eye(gl, dtype=bool)

    def in_proj(bb):
        x = bb.reshape(nj, gl, p, ch).transpose(0, 1, 3, 2)
        x = jnp.where(eye[None, :, None, :, None], x[:, :, :, None, :], 0.0)
        return x.reshape(nj, gl * ch, gl * p)

    def out_proj(c):
        x = c.reshape(nj, gl, ch, p).transpose(0, 1, 3, 2)
        x = jnp.where(eye[None, :, None, :, None], x[:, :, :, None, :], 0.0)
        return x.reshape(nj, gl * p, gl * ch)

    b_bd = jnp.concatenate([in_proj(bb_re), in_proj(bb_im)], axis=-1).astype(BF16)
    return b_bd, out_proj(c_re).astype(BF16), out_proj(c_im).astype(BF16)


def _ssm_kernel(u_ref, s0r_ref, s0i_ref, bbd_ref, cre_ref, cim_ref, pwr_ref, pwi_ref, d_ref, wg_ref,
                o_ref, sr_out_ref, si_out_ref, sre, sim, car_r, car_i, y_scr, *, tc, last_row, scan_w):
    ti = pl.program_id(1)
    nj, cin, two_w = bbd_ref.shape
    w = two_w // 2
    n_state = nj * w

    @pl.when(ti == 0)
    def _load_state():
        car_r[...] = s0r_ref[...]
        car_i[...] = s0i_ref[...]

    u = u_ref[...]
    ub = u.astype(BF16)
    for j in range(nj):
        r = jnp.dot(ub[:, j * cin:(j + 1) * cin], bbd_ref[j], preferred_element_type=F32)
        sre[:, j * w:(j + 1) * w] = r[:, :w]
        sim[:, j * w:(j + 1) * w] = r[:, w:]

    row = lax.broadcasted_iota(jnp.int32, (SUBLANE, scan_w), 0)
    for c0 in range(0, n_state, scan_w):
        cs = slice(c0, c0 + scan_w)
        a_re = [jnp.broadcast_to(pwr_ref[kk:kk + 1, cs], (SUBLANE, scan_w)) for kk in (0, 1, 3)]
        a_im = [jnp.broadcast_to(pwi_ref[kk:kk + 1, cs], (SUBLANE, scan_w)) for kk in (0, 1, 3)]
        a8_re = pwr_ref[:, cs]
        a8_im = pwi_ref[:, cs]

        def body(i, carry):
            c_re, c_im = carry
            r0 = pl.multiple_of(i * SUBLANE, SUBLANE)
            x_re = sre[pl.ds(r0, SUBLANE), cs]
            x_im = sim[pl.ds(r0, SUBLANE), cs]
            for step, shift in enumerate((1, 2, 4)):
                keep = row >= shift
                s_re = jnp.where(keep, pltpu.roll(x_re, shift, 0), 0.0)
                s_im = jnp.where(keep, pltpu.roll(x_im, shift, 0), 0.0)
                x_re, x_im = (x_re + (a_re[step] * s_re - a_im[step] * s_im),
                              x_im + (a_re[step] * s_im + a_im[step] * s_re))
            x_re, x_im = (x_re + (a8_re * c_re - a8_im * c_im),
                          x_im + (a8_re * c_im + a8_im * c_re))
            sre[pl.ds(r0, SUBLANE), cs] = x_re
            sim[pl.ds(r0, SUBLANE), cs] = x_im
            return (jnp.broadcast_to(x_re[SUBLANE - 1:SUBLANE], (SUBLANE, scan_w)),
                    jnp.broadcast_to(x_im[SUBLANE - 1:SUBLANE], (SUBLANE, scan_w)))

        init = (jnp.broadcast_to(car_r[:, cs], (SUBLANE, scan_w)),
                jnp.broadcast_to(car_i[:, cs], (SUBLANE, scan_w)))
        lax.fori_loop(0, tc // SUBLANE, body, init)

    car_r[...] = sre[tc - 1:tc, :]
    car_i[...] = sim[tc - 1:tc, :]

    @pl.when(ti == pl.num_programs(1) - 1)
    def _final_state():
        sr_out_ref[...] = sre[last_row:last_row + 1, :]
        si_out_ref[...] = sim[last_row:last_row + 1, :]

    for j in range(nj):
        y_scr[:, j * cin:(j + 1) * cin] = (
            jnp.dot(sre[:, j * w:(j + 1) * w].astype(BF16), cre_ref[j], preferred_element_type=F32)
            - jnp.dot(sim[:, j * w:(j + 1) * w].astype(BF16), cim_ref[j], preferred_element_type=F32))
    y = y_scr[...] + d_ref[...] * u
    gg = jax.nn.gelu(y)
    gate = jnp.dot(gg.astype(BF16), wg_ref[...], preferred_element_type=F32)
    o_ref[...] = (gg * jax.nn.sigmoid(gate)).astype(o_ref.dtype)


def ssm_mix(h3, col_block, seq_len, s0_re, s0_im, prep, d_skip, w_glu):
    b_bd, c_re_bd, c_im_bd, pw_re, pw_im = prep
    b, l, _ = h3.shape
    nj, cin, two_w = b_bd.shape
    width = nj * cin
    n_state = nj * two_w // 2
    tc = _tile(l, 256, SUBLANE)
    nt = l // tc
    last_row = (seq_len - 1) % tc
    kernel = functools.partial(_ssm_kernel, tc=tc, last_row=last_row, scan_w=256)
    const3 = lambda bi, ti: (0, 0, 0)
    const2 = lambda bi, ti: (0, 0)
    out, s_re, s_im = pl.pallas_call(
        kernel,
        grid=(b, nt),
        in_specs=[pl.BlockSpec((None, tc, width), lambda bi, ti: (bi, ti, col_block)),
                  pl.BlockSpec((None, 1, n_state), lambda bi, ti: (bi, 0, 0)),
                  pl.BlockSpec((None, 1, n_state), lambda bi, ti: (bi, 0, 0)),
                  pl.BlockSpec(b_bd.shape, const3),
                  pl.BlockSpec(c_re_bd.shape, const3),
                  pl.BlockSpec(c_im_bd.shape, const3),
                  pl.BlockSpec(pw_re.shape, const2),
                  pl.BlockSpec(pw_im.shape, const2),
                  pl.BlockSpec((1, width), const2),
                  pl.BlockSpec(w_glu.shape, const2)],
        out_specs=[pl.BlockSpec((None, tc, width), lambda bi, ti: (bi, ti, 0)),
                   pl.BlockSpec((None, 1, n_state), lambda bi, ti: (bi, 0, 0)),
                   pl.BlockSpec((None, 1, n_state), lambda bi, ti: (bi, 0, 0))],
        out_shape=[jax.ShapeDtypeStruct((b, l, width), BF16),
                   jax.ShapeDtypeStruct((b, 1, n_state), F32),
                   jax.ShapeDtypeStruct((b, 1, n_state), F32)],
        scratch_shapes=[pltpu.VMEM((tc, n_state), F32), pltpu.VMEM((tc, n_state), F32),
                        pltpu.VMEM((1, n_state), F32), pltpu.VMEM((1, n_state), F32),
                        pltpu.VMEM((tc, width), F32)],
        compiler_params=_params("parallel", "arbitrary"),
        name="ssm_mix",
    )(h3, s0_re.reshape(b, 1, n_state), s0_im.reshape(b, 1, n_state), b_bd, c_re_bd, c_im_bd,
      pw_re, pw_im, d_skip.reshape(1, width), w_glu)
    return out, s_re.reshape(b, n_state), s_im.reshape(b, n_state)


def _pool_kernel(u_ref, pre_ref, w_ref, sc_ref, o_ref, xp, *, tc, n_prev):
    ti = pl.program_id(1)
    n_win, grp, _ = w_ref.shape

    @pl.when(ti == 0)
    def _prefix():
        xp[0:POOL_HIST, :] = pre_ref[...]

    @pl.when(ti > 0)
    def _history():
        xp[0:POOL_HIST, :] = xp[tc:tc + POOL_HIST, :]

    u = u_ref[...]
    xp[POOL_HIST:POOL_HIST + tc, :] = u
    t_idx = ti * tc + lax.broadcasted_iota(jnp.int32, (tc, 1), 0) + 1
    for gi, win in enumerate(POOL_WINDOWS[:n_win]):
        cs = slice(gi * grp, (gi + 1) * grp)
        acc = u[:, cs]
        for dlt in range(1, win):
            acc = acc + xp[POOL_HIST - dlt:POOL_HIST - dlt + tc, cs]
        count = jnp.minimum(n_prev + t_idx, win).astype(F32)
        m = acc / count - u[:, cs]
        y = jnp.dot(m.astype(BF16), w_ref[gi], preferred_element_type=F32)
        o_ref[:, cs] = (y * sc_ref[:, cs]).astype(o_ref.dtype)


def pool_mix(h3, col_block, prefix, n_prev, pool_w, pool_scale):
    b, l, _ = h3.shape
    width = pool_scale.shape[0]
    tc = _tile(l, 256, SUBLANE)
    kernel = functools.partial(_pool_kernel, tc=tc, n_prev=n_prev)
    return pl.pallas_call(
        kernel,
        grid=(b, l // tc),
        in_specs=[pl.BlockSpec((None, tc, width), lambda bi, ti: (bi, ti, col_block)),
                  pl.BlockSpec((None, POOL_HIST, width), lambda bi, ti: (bi, 0, 0)),
                  pl.BlockSpec(pool_w.shape, lambda bi, ti: (0, 0, 0)),
                  pl.BlockSpec((1, width), lambda bi, ti: (0, 0))],
        out_specs=pl.BlockSpec((None, tc, width), lambda bi, ti: (bi, ti, 0)),
        out_shape=jax.ShapeDtypeStruct((b, l, width), BF16),
        scratch_shapes=[pltpu.VMEM((POOL_HIST + tc, width), F32)],
        compiler_params=_params("parallel", "arbitrary"),
        name="pool_mix",
    )(h3, prefix, pool_w, pool_scale.reshape(1, width))


def _kv_export_kernel(*refs, depth, slabs, hd):
    src = refs[:2 * depth]
    ko_ref, vo_ref = refs[2 * depth:]
    d = pl.program_id(0)
    tm = src[0].shape[0]
    for l in range(depth):
        @pl.when(d == l)
        def _layer(l=l):
            for j in range(slabs):
                ko_ref[pl.ds(j, tm, stride=slabs), :] = src[2 * l][:, j * hd:(j + 1) * hd]
                vo_ref[pl.ds(j, tm, stride=slabs), :] = src[2 * l + 1][:, j * hd:(j + 1) * hd]


def kv_export(h_list, attn_w, hd):
    depth = len(h_list)
    m = h_list[0].shape[0]
    slabs = attn_w // hd
    tm = _tile(m, 128, SUBLANE)
    nt = m // tm

    def src_spec(l, col):
        def index(d, i):
            return (jnp.where(d == l, i, jnp.where(d < l, 0, nt - 1)), col)
        return pl.BlockSpec((tm, attn_w), index)

    in_specs, args = [], []
    for l, h in enumerate(h_list):
        in_specs += [src_spec(l, 1), src_spec(l, 2)]
        args += [h, h]
    out_spec = pl.BlockSpec((None, tm * slabs, hd), lambda d, i: (d, i, 0))
    out_shape = jax.ShapeDtypeStruct((depth, m * slabs, hd), F32)
    return pl.pallas_call(
        functools.partial(_kv_export_kernel, depth=depth, slabs=slabs, hd=hd),
        grid=(depth, nt),
        in_specs=in_specs,
        out_specs=[out_spec, out_spec],
        out_shape=[out_shape, out_shape],
        compiler_params=_params("arbitrary", "arbitrary"),
        name="kv_export",
    )(*args)


def _ffn(x, xn, g_post, g_next, wg, wu, wd):
    return resnorm(x, down(gateup(xn, wg, wu), wd), g_post, 0.5, g_next)


def kernel(x_prompt, x_sample, cache_k, cache_v, state_ssm_re, state_ssm_im, state_pool, page_table,
           norm_g, w_ffn_gate, w_ffn_up, w_ffn_down, w_in, w_out, rel_bias, diff_lambda, diff_subln,
           ssm_lam_re, ssm_lam_im, ssm_log_dt, ssm_b_re, ssm_b_im, ssm_c_re, ssm_c_im, ssm_d, ssm_w_glu,
           pool_w, pool_scale):
    bp, seq, d_model = x_prompt.shape
    db, dec_seq, _ = x_sample.shape
    depth = norm_g.shape[0]
    _, n_pool, page, nh, _, hd = cache_k.shape
    vd = 2 * hd
    attn_w = nh * vd
    ssm_w = ssm_d.shape[1]
    pool_wd = pool_scale.shape[1]
    n_groups, n_state_g = ssm_lam_re.shape[1:]
    n_state = n_groups * n_state_g
    n_buf = state_pool.shape[2]
    past_len = page_table.shape[1] * page
    assert attn_w % vd == 0 and ssm_w == vd * (ssm_w // vd) and pool_wd == ssm_w
    ssm_blk = 3 * attn_w // ssm_w
    pool_blk = (3 * attn_w + ssm_w) // pool_wd
    dec_pad = _tile(page, page, SUBLANE)
    t_attn = _tile(seq, 512, LANE)

    lam_init = [0.8 - 0.6 * math.exp(-0.3 * l) for l in range(depth)]
    lams = diff_lambdas(diff_lambda, lam_init)
    d_tiles, dec_tab = bias_tables(rel_bias, t_attn, dec_seq)
    far = dec_tab[:, 0, 0, 0]
    dec_bias = dec_tab.transpose(1, 0, 2, 3).reshape(3, nh * SUBLANE, LANE)
    cache_k4 = cache_k.reshape(depth, n_pool, page * nh * 2, hd)
    cache_v4 = cache_v.reshape(depth, n_pool, page * nh * 2, hd)
    eye_hc = jnp.eye(2 * nh, dtype=bool)

    xp = x_prompt.reshape(bp * seq, d_model)
    xs = x_sample.reshape(db * dec_seq, d_model)
    xpn = rmsnorm_cast(xp, norm_g[0, 0])
    xsn = rmsnorm_cast(xs, norm_g[0, 0])
    h_prompt = []
    outs = [[] for _ in range(10)]
    for l in range(depth):
        g = norm_g[l]
        g_after = norm_g[min(l + 1, depth - 1), 0]
        par = jnp.stack([lams[l], jnp.asarray(1.0 - lam_init[l], F32)])
        pw_re, pw_im, bb_re, bb_im = ssm_prep(ssm_lam_re[l], ssm_lam_im[l], ssm_log_dt[l], ssm_b_re[l], ssm_b_im[l])
        b_bd, c_re_bd, c_im_bd = _ssm_blockdiag(bb_re, bb_im, ssm_c_re[l], ssm_c_im[l])
        prep = (b_bd, c_re_bd, c_im_bd, pw_re, pw_im)
        w_glu = ssm_w_glu[l].astype(BF16)
        pw = pool_w[l].astype(BF16)

        xp, xpn = _ffn(xp, xpn, g[1], g[2], w_ffn_gate[l, 0], w_ffn_up[l, 0], w_ffn_down[l, 0])
        h = matmul(xpn, w_in[l])
        h_prompt.append(h)
        h3 = h.reshape(bp, seq, -1)
        a_out = attn_prompt(h3, par, far, d_tiles, diff_subln[l], t_attn)
        zeros_state = jnp.zeros((bp, n_state), F32)
        s_out, s_re, s_im = ssm_mix(h3, ssm_blk, seq, zeros_state, zeros_state, prep, ssm_d[l], w_glu)
        p_out = pool_mix(h3, pool_blk, jnp.zeros((bp, POOL_HIST, pool_wd), F32), 0, pw, pool_scale[l])
        mixed = jnp.concatenate([a_out, s_out, p_out], axis=-1).reshape(bp * seq, -1)
        xp, xpn = resnorm(xp, matmul(mixed, w_out[l]), g[3], 1.0, g[4])
        xp, xpn = _ffn(xp, xpn, g[5], g_after, w_ffn_gate[l, 1], w_ffn_up[l, 1], w_ffn_down[l, 1])
        outs[2].append(s_re.reshape(bp, n_groups, n_state_g))
        outs[3].append(s_im.reshape(bp, n_groups, n_state_g))
        outs[4].append(h3[:, seq - n_buf:, 3 * attn_w + ssm_w:])

        xs, xsn = _ffn(xs, xsn, g[1], g[2], w_ffn_gate[l, 0], w_ffn_up[l, 0], w_ffn_down[l, 0])
        hs = matmul(xsn, w_in[l]).reshape(db, dec_seq, -1)
        q = hs[..., :attn_w].reshape(db, dec_seq, 2 * nh, hd).transpose(0, 2, 1, 3)
        qbd = jnp.where(eye_hc[None, :, None, :, None], q[:, :, :, None, :], 0.0)
        qbd = qbd.reshape(db, 2 * nh * dec_seq, attn_w).astype(BF16)
        k_new = hs[..., attn_w:2 * attn_w]
        v_new = hs[..., 2 * attn_w:3 * attn_w]
        pad_tok = ((0, 0), (0, dec_pad - dec_seq), (0, 0))
        a_rows = attn_sample(page_table, par, qbd, cache_k4, cache_v4, l, jnp.pad(k_new, pad_tok),
                             jnp.pad(v_new, pad_tok), dec_bias, diff_subln[l], dec_seq)
        a_s = a_rows.reshape(db, nh, 2, dec_seq, vd)[:, :, 0].transpose(0, 2, 1, 3).reshape(db, dec_seq, attn_w)
        l_pad = -(-dec_seq // SUBLANE) * SUBLANE
        hs_pad = jnp.pad(hs, ((0, 0), (0, l_pad - dec_seq), (0, 0)))
        s_s, ss_re, ss_im = ssm_mix(hs_pad, ssm_blk, dec_seq, state_ssm_re[l].reshape(db, n_state),
                                    state_ssm_im[l].reshape(db, n_state), prep, ssm_d[l], w_glu)
        prefix = jnp.pad(state_pool[l], ((0, 0), (POOL_HIST - n_buf, 0), (0, 0)))
        p_s = pool_mix(hs_pad, pool_blk, prefix, past_len, pw, pool_scale[l])
        mixed_s = jnp.concatenate([a_s.astype(BF16), s_s[:, :dec_seq], p_s[:, :dec_seq]], axis=-1)
        xs, xsn = resnorm(xs, matmul(mixed_s.reshape(db * dec_seq, -1), w_out[l]), g[3], 1.0, g[4])
        xs, xsn = _ffn(xs, xsn, g[5], g_after, w_ffn_gate[l, 1], w_ffn_up[l, 1], w_ffn_down[l, 1])
        up_s = hs[..., 3 * attn_w + ssm_w:]
        outs[5].append(k_new.reshape(db, dec_seq, nh, 2, hd))
        outs[6].append(v_new.reshape(db, dec_seq, nh, vd))
        outs[7].append(ss_re.reshape(db, n_groups, n_state_g))
        outs[8].append(ss_im.reshape(db, n_groups, n_state_g))
        outs[9].append(jnp.concatenate([state_pool[l], up_s], axis=1)[:, -n_buf:])
    k_prompt, v_prompt = kv_export(h_prompt, attn_w, hd)
    return (xp.reshape(bp, seq, d_model), xs.reshape(db, dec_seq, d_model),
            k_prompt.reshape(depth, bp, seq, nh, 2, hd), v_prompt.reshape(depth, bp, seq, nh, vd),
            *[jnp.stack(o) for o in outs[2:]])
```

```python
import functools
import math

import jax
import jax.numpy as jnp
from jax import lax
from jax.experimental import pallas as pl
from jax.experimental.pallas import tpu as pltpu

F32 = jnp.float32
BF16 = jnp.bfloat16

RMS_EPS = 1e-6
N_HEADS = 8
NUM_BUCKETS = 32
MAX_DISTANCE = 128
POOL_WINDOWS = (2, 4, 8, 16)
POOL_HIST = 16
SSM_CH = 16
LANE = 128
SUBLANE = 8
VMEM_LIMIT = 56 * 1024 * 1024
ROW_TILE = 2048


def _tile(dim, pref, align):
    t = min(pref, dim)
    t -= t % align
    while t >= align:
        if dim % t == 0:
            return t
        t -= align
    return dim


def _params(*sem):
    return pltpu.CompilerParams(dimension_semantics=sem, vmem_limit_bytes=VMEM_LIMIT)


def _rmsnorm_kernel(x_ref, g_ref, o_ref):
    x = x_ref[...]
    ms = jnp.mean(x * x, axis=-1, keepdims=True)
    o_ref[...] = (x * lax.rsqrt(ms + RMS_EPS) * g_ref[...]).astype(o_ref.dtype)


def rmsnorm_cast(x, g):
    m, d = x.shape
    tm = _tile(m, 256, SUBLANE)
    return pl.pallas_call(
        _rmsnorm_kernel,
        grid=(m // tm,),
        in_specs=[pl.BlockSpec((tm, d), lambda i: (i, 0)),
                  pl.BlockSpec((1, d), lambda i: (0, 0))],
        out_specs=pl.BlockSpec((tm, d), lambda i: (i, 0)),
        out_shape=jax.ShapeDtypeStruct((m, d), BF16),
        compiler_params=_params("parallel"),
        name="rmsnorm_cast",
    )(x, g.reshape(1, d))


def _resnorm_kernel(x_ref, y_ref, g_ref, gn_ref, o_ref, on_ref, *, scale):
    y = y_ref[...]
    ms = jnp.mean(y * y, axis=-1, keepdims=True)
    x = x_ref[...] + scale * (y * lax.rsqrt(ms + RMS_EPS) * g_ref[...])
    o_ref[...] = x
    ms = jnp.mean(x * x, axis=-1, keepdims=True)
    on_ref[...] = (x * lax.rsqrt(ms + RMS_EPS) * gn_ref[...]).astype(on_ref.dtype)


def resnorm(x, y, g, scale, g_next):
    m, d = x.shape
    tm = _tile(m, 256, 16)
    row = pl.BlockSpec((tm, d), lambda i: (i, 0))
    vec = pl.BlockSpec((1, d), lambda i: (0, 0))
    return pl.pallas_call(
        functools.partial(_resnorm_kernel, scale=scale),
        grid=(m // tm,),
        in_specs=[row, row, vec, vec],
        out_specs=[row, row],
        out_shape=[jax.ShapeDtypeStruct((m, d), F32), jax.ShapeDtypeStruct((m, d), BF16)],
        compiler_params=_params("parallel"),
        name="resnorm",
    )(x, y, g.reshape(1, d), g_next.reshape(1, d))


def _gateup_kernel(a_ref, wg_ref, wu_ref, o_ref):
    a = a_ref[...]
    g = jnp.dot(a, wg_ref[...].astype(BF16), preferred_element_type=F32)
    u = jnp.dot(a, wu_ref[...].astype(BF16), preferred_element_type=F32)
    o_ref[...] = (g * jax.nn.sigmoid(g) * u).astype(o_ref.dtype)


def _weight_spec(lead, rows, cols, index):
    return pl.BlockSpec((None,) * len(lead) + (rows, cols), lambda i, j: tuple(lead) + index(i, j))


def gateup(a, wg, wu, lead):
    m, k = a.shape
    n = wg.shape[-1]
    tm = _tile(m, ROW_TILE, 16)
    tn = _tile(n, 256, LANE)
    return pl.pallas_call(
        _gateup_kernel,
        grid=(m // tm, n // tn),
        in_specs=[pl.BlockSpec((tm, k), lambda i, j: (i, 0), pipeline_mode=pl.Buffered(1)),
                  _weight_spec(lead, k, tn, lambda i, j: (0, j)),
                  _weight_spec(lead, k, tn, lambda i, j: (0, j))],
        out_specs=pl.BlockSpec((None, tm, tn), lambda i, j: (j, i, 0)),
        out_shape=jax.ShapeDtypeStruct((n // tn, m, tn), BF16),
        compiler_params=_params("parallel", "arbitrary"),
        name="gateup",
    )(a, wg, wu)


def _down_kernel(a_ref, w_ref, o_ref, *, n_chunk):
    k = pl.program_id(1)
    a = a_ref[...]
    n = o_ref.shape[1]

    @pl.when(k == 0)
    def _first():
        for c in range(0, n, n_chunk):
            o_ref[:, c:c + n_chunk] = jnp.dot(a, w_ref[:, c:c + n_chunk].astype(BF16),
                                              preferred_element_type=F32)

    @pl.when(k > 0)
    def _rest():
        for c in range(0, n, n_chunk):
            o_ref[:, c:c + n_chunk] += jnp.dot(a, w_ref[:, c:c + n_chunk].astype(BF16),
                                               preferred_element_type=F32)


def down(act, w, lead):
    nk, m, tk = act.shape
    n = w.shape[-1]
    tm = _tile(m, ROW_TILE, 16)
    n_chunk = _tile(n, 512, LANE)
    return pl.pallas_call(
        functools.partial(_down_kernel, n_chunk=n_chunk),
        grid=(m // tm, nk),
        in_specs=[pl.BlockSpec((None, tm, tk), lambda i, kk: (kk, i, 0)),
                  _weight_spec(lead, tk, n, lambda i, kk: (kk, 0))],
        out_specs=pl.BlockSpec((tm, n), lambda i, kk: (i, 0), pipeline_mode=pl.Buffered(1)),
        out_shape=jax.ShapeDtypeStruct((m, n), F32),
        compiler_params=_params("parallel", "arbitrary"),
        name="down",
    )(act, w)


def _mm_kernel(a_ref, w_ref, o_ref):
    o_ref[...] = jnp.dot(a_ref[...], w_ref[...].astype(BF16), preferred_element_type=F32)


def matmul(a, w, lead):
    m, k = a.shape
    n = w.shape[-1]
    tm = _tile(m, ROW_TILE, 16)
    tn = _tile(n, 512, LANE)
    return pl.pallas_call(
        _mm_kernel,
        grid=(m // tm, n // tn),
        in_specs=[pl.BlockSpec((tm, k), lambda i, j: (i, 0), pipeline_mode=pl.Buffered(1)),
                  _weight_spec(lead, k, tn, lambda i, j: (0, j))],
        out_specs=pl.BlockSpec((tm, tn), lambda i, j: (i, j)),
        out_shape=jax.ShapeDtypeStruct((m, n), F32),
        compiler_params=_params("parallel", "arbitrary"),
        name="matmul",
    )(a, w)


def _t5_bias(rel, rb_ref, h):
    n = jnp.maximum(rel, 0)
    max_exact = NUM_BUCKETS // 2
    nf = jnp.maximum(n, 1).astype(F32)
    large = max_exact + (jnp.log(nf / max_exact) / math.log(MAX_DISTANCE / max_exact)
                         * (NUM_BUCKETS - max_exact)).astype(jnp.int32)
    large = jnp.minimum(large, NUM_BUCKETS - 1)
    bucket = jnp.where(n < max_exact, n, large)
    out = jnp.zeros(rel.shape, F32)
    for b in range(NUM_BUCKETS):
        out = jnp.where(bucket == b, rb_ref[b, h], out)
    return out


def _bias_kernel(rb_ref, d_ref, dec_ref, *, t, dec_seq):
    h = pl.program_id(0)
    r = lax.broadcasted_iota(jnp.int32, (t, t), 0)
    c = lax.broadcasted_iota(jnp.int32, (t, t), 1)
    d_ref[0] = jnp.where(r >= c, _t5_bias(r - c, rb_ref, h), -jnp.inf)
    d_ref[1] = _t5_bias(t + r - c, rb_ref, h)
    row = lax.broadcasted_iota(jnp.int32, (SUBLANE, LANE), 0)
    col = lax.broadcasted_iota(jnp.int32, (SUBLANE, LANE), 1)
    tq = row % dec_seq
    dec_ref[0] = _t5_bias(jnp.full((SUBLANE, LANE), 2 * MAX_DISTANCE, jnp.int32), rb_ref, h)
    dec_ref[1] = _t5_bias(LANE + tq - col, rb_ref, h)
    dec_ref[2] = jnp.where((col <= tq) & (col < dec_seq), _t5_bias(tq - col, rb_ref, h), -jnp.inf)


def bias_tables(rel_bias, t, dec_seq):
    nb, nh = rel_bias.shape
    return pl.pallas_call(
        functools.partial(_bias_kernel, t=t, dec_seq=dec_seq),
        grid=(nh,),
        in_specs=[pl.BlockSpec(memory_space=pltpu.SMEM)],
        out_specs=[pl.BlockSpec((None, 2, t, t), lambda h: (h, 0, 0, 0)),
                   pl.BlockSpec((None, 3, SUBLANE, LANE), lambda h: (h, 0, 0, 0))],
        out_shape=[jax.ShapeDtypeStruct((nh, 2, t, t), F32),
                   jax.ShapeDtypeStruct((nh, 3, SUBLANE, LANE), F32)],
        compiler_params=_params("arbitrary"),
        name="bias_tables",
    )(rel_bias)


def _lambda_kernel(lp_ref, init_ref, o_ref):
    lp = lp_ref[...]
    s1 = jnp.sum(lp[:, 0, :] * lp[:, 1, :], axis=-1, keepdims=True)
    s2 = jnp.sum(lp[:, 2, :] * lp[:, 3, :], axis=-1, keepdims=True)
    o_ref[...] = jnp.broadcast_to(jnp.exp(s1) - jnp.exp(s2) + init_ref[...], o_ref.shape)


def diff_lambdas(diff_lambda, lam_init):
    depth = diff_lambda.shape[0]
    out = pl.pallas_call(
        _lambda_kernel,
        out_shape=jax.ShapeDtypeStruct((depth, LANE), F32),
        name="diff_lambdas",
    )(diff_lambda, jnp.asarray(lam_init, F32).reshape(depth, 1))
    return out[:, 0]


def _attn_kernel(par_ref, far_ref, q_ref, k_ref, v_ref, d_ref, g_ref, o_ref,
                 m_scr, l_scr, acc_scr, *, scale, hd):
    h = pl.program_id(1)
    qi = pl.program_id(2)
    ki = pl.program_id(3)

    @pl.when(ki == 0)
    def _init():
        m_scr[...] = jnp.full(m_scr.shape, -jnp.inf, F32)
        l_scr[...] = jnp.zeros(l_scr.shape, F32)
        acc_scr[...] = jnp.zeros(acc_scr.shape, F32)

    def step(bias):
        q = q_ref[...].astype(BF16)
        k = k_ref[...].astype(BF16)
        v = v_ref[...].astype(BF16)
        for c in range(2):
            s = lax.dot_general(q[:, c * hd:(c + 1) * hd], k[:, c * hd:(c + 1) * hd],
                                (((1,), (1,)), ((), ())), preferred_element_type=F32)
            s = s * scale + bias
            m_prev = m_scr[c]
            m_new = jnp.maximum(m_prev, jnp.max(s, axis=-1, keepdims=True))
            alpha = jnp.exp(m_prev - m_new)
            p = jnp.exp(s - m_new)
            l_scr[c] = alpha * l_scr[c] + jnp.sum(p, axis=-1, keepdims=True)
            acc_scr[c] = alpha * acc_scr[c] + jnp.dot(p.astype(BF16), v, preferred_element_type=F32)
            m_scr[c] = m_new

    @pl.when(ki == qi)
    def _diag():
        step(d_ref[0])

    @pl.when(ki == qi - 1)
    def _sub():
        step(d_ref[1])

    @pl.when(ki < qi - 1)
    def _far():
        step(far_ref[h])

    @pl.when(ki == qi)
    def _finish():
        o = acc_scr[0] / l_scr[0] - par_ref[0] * (acc_scr[1] / l_scr[1])
        ms = jnp.mean(o * o, axis=-1, keepdims=True)
        o_ref[...] = ((o * lax.rsqrt(ms + RMS_EPS) * g_ref[...]) * par_ref[1]).astype(o_ref.dtype)


def attn_prompt(h3, par, far, d_tiles, subln_g, t):
    b, s, _ = h3.shape
    nh = d_tiles.shape[0]
    hd = subln_g.shape[0] // 2
    vd = 2 * hd
    nt = s // t
    kernel = functools.partial(_attn_kernel, scale=hd ** -0.5, hd=hd)
    return pl.pallas_call(
        kernel,
        grid=(b, nh, nt, nt),
        in_specs=[pl.BlockSpec(memory_space=pltpu.SMEM),
                  pl.BlockSpec(memory_space=pltpu.SMEM),
                  pl.BlockSpec((None, t, vd), lambda bi, hi, qi, ki: (bi, qi, hi)),
                  pl.BlockSpec((None, t, vd), lambda bi, hi, qi, ki: (bi, jnp.minimum(ki, qi), nh + hi)),
                  pl.BlockSpec((None, t, vd), lambda bi, hi, qi, ki: (bi, jnp.minimum(ki, qi), 2 * nh + hi)),
                  pl.BlockSpec((None, 2, t, t), lambda bi, hi, qi, ki: (hi, 0, 0, 0)),
                  pl.BlockSpec((1, vd), lambda bi, hi, qi, ki: (0, 0))],
        out_specs=pl.BlockSpec((None, t, vd), lambda bi, hi, qi, ki: (bi, qi, hi)),
        out_shape=jax.ShapeDtypeStruct((b, s, nh * vd), BF16),
        scratch_shapes=[pltpu.VMEM((2, t, 1), F32), pltpu.VMEM((2, t, 1), F32),
                        pltpu.VMEM((2, t, vd), F32)],
        compiler_params=_params("parallel", "parallel", "parallel", "arbitrary"),
        name="attn_prompt",
    )(par, far, h3, h3, h3, d_tiles, subln_g.reshape(1, vd))


def _dec_attn_kernel(pt_ref, par_ref, q_ref, kc_ref, vc_ref, kn_ref, vn_ref, bias_ref, bias_new_ref, g_ref,
                     o_ref, m_scr, l_scr, acc_scr, *, scale):
    p = pl.program_id(1)
    last = pl.num_programs(1) - 1
    n_q = q_ref.shape[1]

    @pl.when(p == 0)
    def _init():
        m_scr[...] = jnp.full(m_scr.shape, -jnp.inf, F32)
        l_scr[...] = jnp.zeros(l_scr.shape, F32)
        acc_scr[...] = jnp.zeros(acc_scr.shape, F32)

    def step(keys, vb, bias):
        s = jnp.concatenate(
            [lax.dot_general(q_ref[c], keys[c], (((1,), (1,)), ((), ())), preferred_element_type=F32) * scale
             + bias for c in range(2)], axis=0)
        m_prev = m_scr[...]
        m_new = jnp.maximum(m_prev, jnp.max(s, axis=-1, keepdims=True))
        alpha = jnp.exp(m_prev - m_new)
        pr = jnp.exp(s - m_new)
        l_scr[...] = alpha * l_scr[...] + jnp.sum(pr, axis=-1, keepdims=True)
        m_scr[...] = m_new
        acc_scr[...] = alpha * acc_scr[...] + jnp.dot(pr.astype(BF16), vb, preferred_element_type=F32)

    def cached_step(bias):
        n_kv = vc_ref.shape[0]
        keys = [kc_ref[pl.ds(c, n_kv, stride=2), :].astype(BF16) for c in range(2)]
        step(keys, vc_ref[...].astype(BF16), bias)

    @pl.when(p < last)
    def _far():
        cached_step(bias_ref[0])

    @pl.when(p == last)
    def _tail():
        cached_step(bias_ref[1])
        step([kn_ref[c].astype(BF16) for c in range(2)], vn_ref[...].astype(BF16), bias_new_ref[...])
        o_all = acc_scr[...] / l_scr[...]
        o = o_all[:n_q] - par_ref[0] * o_all[n_q:]
        ms = jnp.mean(o * o, axis=-1, keepdims=True)
        o_ref[...] = (o * lax.rsqrt(ms + RMS_EPS) * g_ref[...]) * par_ref[1]


def attn_sample(page_table, par, q, cache_k, cache_v, layer, k_new, v_new, bias, bias_new, subln_g):
    db, n_pages = page_table.shape
    _, _, k_rows, hd = cache_k.shape
    _, _, v_rows, vd = cache_v.shape
    n_q = q.shape[2]
    n_new = v_new.shape[1]
    kernel = functools.partial(_dec_attn_kernel, scale=hd ** -0.5)
    grid_spec = pltpu.PrefetchScalarGridSpec(
        num_scalar_prefetch=1,
        grid=(db, n_pages),
        in_specs=[pl.BlockSpec(memory_space=pltpu.SMEM),
                  pl.BlockSpec((None, 2, n_q, hd), lambda bi, pi, pt: (bi, 0, 0, 0)),
                  pl.BlockSpec((None, None, k_rows, hd),
                               lambda bi, pi, pt: (layer, pt[bi * n_pages + pi], 0, 0)),
                  pl.BlockSpec((None, None, v_rows, vd),
                               lambda bi, pi, pt: (layer, pt[bi * n_pages + pi], 0, 0)),
                  pl.BlockSpec((None, 2, n_new, hd), lambda bi, pi, pt: (bi, 0, 0, 0)),
                  pl.BlockSpec((None, n_new, vd), lambda bi, pi, pt: (bi, 0, 0)),
                  pl.BlockSpec((2, n_q, v_rows), lambda bi, pi, pt: (0, 0, 0)),
                  pl.BlockSpec((n_q, n_new), lambda bi, pi, pt: (0, 0)),
                  pl.BlockSpec((1, vd), lambda bi, pi, pt: (0, 0))],
        out_specs=pl.BlockSpec((None, n_q, vd), lambda bi, pi, pt: (bi, 0, 0)),
        scratch_shapes=[pltpu.VMEM((2 * n_q, 1), F32), pltpu.VMEM((2 * n_q, 1), F32),
                        pltpu.VMEM((2 * n_q, vd), F32)],
    )
    return pl.pallas_call(
        kernel,
        grid_spec=grid_spec,
        out_shape=jax.ShapeDtypeStruct((db, n_q, vd), F32),
        compiler_params=_params("parallel", "arbitrary"),
        name="attn_sample",
    )(page_table.reshape(-1), par, q, cache_k, cache_v, k_new, v_new, bias, bias_new, subln_g.reshape(1, vd))


def _ssm_prep_kernel(lr_ref, li_ref, ldt_ref, lrw_ref, liw_ref, br_ref, bi_ref,
                     pw_re_ref, pw_im_ref, bb_re_ref, bb_im_ref):
    dt = jnp.exp(ldt_ref[...])
    zr = lr_ref[...] * dt
    zi = li_ref[...] * dt
    for kk in range(SUBLANE):
        mag = jnp.exp((kk + 1.0) * zr)
        pw_re_ref[kk] = mag * jnp.cos((kk + 1.0) * zi)
        pw_im_ref[kk] = mag * jnp.sin((kk + 1.0) * zi)
    lr = lrw_ref[...]
    li = liw_ref[...]
    mag = jnp.exp(lr * dt)
    x = mag * jnp.cos(li * dt) - 1.0
    y = mag * jnp.sin(li * dt)
    den = lr * lr + li * li
    cr = (x * lr + y * li) / den
    ci = (y * lr - x * li) / den
    br = br_ref[...]
    bi = bi_ref[...]
    bb_re_ref[...] = cr * br - ci * bi
    bb_im_ref[...] = cr * bi + ci * br


def ssm_prep(lam_re, lam_im, log_dt, b_re, b_im):
    g, p = lam_re.shape
    ch = b_re.shape[-1]
    wide = lambda a: jnp.repeat(a, ch, axis=-1)
    outs = pl.pallas_call(
        _ssm_prep_kernel,
        out_shape=[jax.ShapeDtypeStruct((SUBLANE, g, p), F32), jax.ShapeDtypeStruct((SUBLANE, g, p), F32),
                   jax.ShapeDtypeStruct((g, p * ch), F32), jax.ShapeDtypeStruct((g, p * ch), F32)],
        name="ssm_prep",
    )(lam_re, lam_im, log_dt.reshape(g, 1), wide(lam_re), wide(lam_im),
      b_re.reshape(g, p * ch), b_im.reshape(g, p * ch))
    pw_re, pw_im, bb_re, bb_im = outs
    return (pw_re.reshape(SUBLANE, g * p), pw_im.reshape(SUBLANE, g * p),
            bb_re.reshape(g, p, ch), bb_im.reshape(g, p, ch))


def _ssm_blockdiag(bb_re, bb_im, c_re, c_im):
    g, p, ch = bb_re.shape
    gl = LANE // ch
    nj = g // gl
    eye = jnp.eye(gl, dtype=bool)

    def in_proj(bb):
        x = bb.reshape(nj, gl, p, ch).transpose(0, 1, 3, 2)
        x = jnp.where(eye[None, :, None, :, None], x[:, :, :, None, :], 0.0)
        return x.reshape(nj, gl * ch, gl * p)

    def out_proj(c):
        x = c.reshape(nj, gl, ch, p).transpose(0, 1, 3, 2)
        x = jnp.where(eye[None, :, None, :, None], x[:, :, :, None, :], 0.0)
        return x.reshape(nj, gl * p, gl * ch)

    b_bd = jnp.concatenate([in_proj(bb_re), in_proj(bb_im)], axis=-1).astype(BF16)
    return b_bd, out_proj(c_re).astype(BF16), out_proj(c_im).astype(BF16)


def _ssm_kernel(u_ref, s0r_ref, s0i_ref, bbd_ref, cre_ref, cim_ref, pwr_ref, pwi_ref, d_ref, wg_ref,
                o_ref, sr_out_ref, si_out_ref, sre, sim, car_r, car_i, y_scr, *, tc, last_row, scan_w):
    ti = pl.program_id(1)
    nj, cin, two_w = bbd_ref.shape
    w = two_w // 2
    n_state = nj * w

    @pl.when(ti == 0)
    def _load_state():
        car_r[...] = s0r_ref[...]
        car_i[...] = s0i_ref[...]

    u = u_ref[...]
    ub = u.astype(BF16)
    for j in range(nj):
        r = jnp.dot(ub[:, j * cin:(j + 1) * cin], bbd_ref[j], preferred_element_type=F32)
        sre[:, j * w:(j + 1) * w] = r[:, :w]
        sim[:, j * w:(j + 1) * w] = r[:, w:]

    row = lax.broadcasted_iota(jnp.int32, (SUBLANE, scan_w), 0)
    for c0 in range(0, n_state, scan_w):
        cs = slice(c0, c0 + scan_w)
        a_re = [jnp.broadcast_to(pwr_ref[kk:kk + 1, cs], (SUBLANE, scan_w)) for kk in (0, 1, 3)]
        a_im = [jnp.broadcast_to(pwi_ref[kk:kk + 1, cs], (SUBLANE, scan_w)) for kk in (0, 1, 3)]
        a8_re = pwr_ref[:, cs]
        a8_im = pwi_ref[:, cs]

        def body(i, carry):
            c_re, c_im = carry
            r0 = pl.multiple_of(i * SUBLANE, SUBLANE)
            x_re = sre[pl.ds(r0, SUBLANE), cs]
            x_im = sim[pl.ds(r0, SUBLANE), cs]
            for step, shift in enumerate((1, 2, 4)):
                keep = row >= shift
                s_re = jnp.where(keep, pltpu.roll(x_re, shift, 0), 0.0)
                s_im = jnp.where(keep, pltpu.roll(x_im, shift, 0), 0.0)
                x_re, x_im = (x_re + (a_re[step] * s_re - a_im[step] * s_im),
                              x_im + (a_re[step] * s_im + a_im[step] * s_re))
            x_re, x_im = (x_re + (a8_re * c_re - a8_im * c_im),
                          x_im + (a8_re * c_im + a8_im * c_re))
            sre[pl.ds(r0, SUBLANE), cs] = x_re
            sim[pl.ds(r0, SUBLANE), cs] = x_im
            return (jnp.broadcast_to(x_re[SUBLANE - 1:SUBLANE], (SUBLANE, scan_w)),
                    jnp.broadcast_to(x_im[SUBLANE - 1:SUBLANE], (SUBLANE, scan_w)))

        init = (jnp.broadcast_to(car_r[:, cs], (SUBLANE, scan_w)),
                jnp.broadcast_to(car_i[:, cs], (SUBLANE, scan_w)))
        lax.fori_loop(0, tc // SUBLANE, body, init)

    car_r[...] = sre[tc - 1:tc, :]
    car_i[...] = sim[tc - 1:tc, :]

    @pl.when(ti == pl.num_programs(1) - 1)
    def _final_state():
        sr_out_ref[...] = sre[last_row:last_row + 1, :]
        si_out_ref[...] = sim[last_row:last_row + 1, :]

    for j in range(nj):
        y_scr[:, j * cin:(j + 1) * cin] = (
            jnp.dot(sre[:, j * w:(j + 1) * w].astype(BF16), cre_ref[j], preferred_element_type=F32)
            - jnp.dot(sim[:, j * w:(j + 1) * w].astype(BF16), cim_ref[j], preferred_element_type=F32))
    y = y_scr[...] + d_ref[...] * u
    gg = jax.nn.gelu(y)
    gate = jnp.dot(gg.astype(BF16), wg_ref[...], preferred_element_type=F32)
    o_ref[...] = (gg * jax.nn.sigmoid(gate)).astype(o_ref.dtype)


def ssm_mix(h3, col_block, seq_len, s0_re, s0_im, prep, d_skip, w_glu):
    b_bd, c_re_bd, c_im_bd, pw_re, pw_im = prep
    b, l, _ = h3.shape
    nj, cin, two_w = b_bd.shape
    width = nj * cin
    n_state = nj * two_w // 2
    tc = _tile(l, 256, SUBLANE)
    nt = l // tc
    last_row = (seq_len - 1) % tc
    kernel = functools.partial(_ssm_kernel, tc=tc, last_row=last_row, scan_w=256)
    const3 = lambda bi, ti: (0, 0, 0)
    const2 = lambda bi, ti: (0, 0)
    out, s_re, s_im = pl.pallas_call(
        kernel,
        grid=(b, nt),
        in_specs=[pl.BlockSpec((None, tc, width), lambda bi, ti: (bi, ti, col_block)),
                  pl.BlockSpec((None, 1, n_state), lambda bi, ti: (bi, 0, 0)),
                  pl.BlockSpec((None, 1, n_state), lambda bi, ti: (bi, 0, 0)),
                  pl.BlockSpec(b_bd.shape, const3),
                  pl.BlockSpec(c_re_bd.shape, const3),
                  pl.BlockSpec(c_im_bd.shape, const3),
                  pl.BlockSpec(pw_re.shape, const2),
                  pl.BlockSpec(pw_im.shape, const2),
                  pl.BlockSpec((1, width), const2),
                  pl.BlockSpec(w_glu.shape, const2)],
        out_specs=[pl.BlockSpec((None, tc, width), lambda bi, ti: (bi, ti, 0)),
                   pl.BlockSpec((None, 1, n_state), lambda bi, ti: (bi, 0, 0)),
                   pl.BlockSpec((None, 1, n_state), lambda bi, ti: (bi, 0, 0))],
        out_shape=[jax.ShapeDtypeStruct((b, l, width), BF16),
                   jax.ShapeDtypeStruct((b, 1, n_state), F32),
                   jax.ShapeDtypeStruct((b, 1, n_state), F32)],
        scratch_shapes=[pltpu.VMEM((tc, n_state), F32), pltpu.VMEM((tc, n_state), F32),
                        pltpu.VMEM((1, n_state), F32), pltpu.VMEM((1, n_state), F32),
                        pltpu.VMEM((tc, width), F32)],
        compiler_params=_params("parallel", "arbitrary"),
        name="ssm_mix",
    )(h3, s0_re.reshape(b, 1, n_state), s0_im.reshape(b, 1, n_state), b_bd, c_re_bd, c_im_bd,
      pw_re, pw_im, d_skip.reshape(1, width), w_glu)
    return out, s_re.reshape(b, n_state), s_im.reshape(b, n_state)


def _pool_kernel(u_ref, pre_ref, w_ref, sc_ref, o_ref, xp, *, tc, n_prev):
    ti = pl.program_id(1)
    n_win, grp, _ = w_ref.shape

    @pl.when(ti == 0)
    def _prefix():
        xp[0:POOL_HIST, :] = pre_ref[...]

    @pl.when(ti > 0)
    def _history():
        xp[0:POOL_HIST, :] = xp[tc:tc + POOL_HIST, :]

    u = u_ref[...]
    xp[POOL_HIST:POOL_HIST + tc, :] = u
    t_idx = ti * tc + lax.broadcasted_iota(jnp.int32, (tc, 1), 0) + 1
    for gi, win in enumerate(POOL_WINDOWS[:n_win]):
        cs = slice(gi * grp, (gi + 1) * grp)
        acc = u[:, cs]
        for dlt in range(1, win):
            acc = acc + xp[POOL_HIST - dlt:POOL_HIST - dlt + tc, cs]
        count = jnp.minimum(n_prev + t_idx, win).astype(F32)
        m = acc / count - u[:, cs]
        y = jnp.dot(m.astype(BF16), w_ref[gi], preferred_element_type=F32)
        o_ref[:, cs] = (y * sc_ref[:, cs]).astype(o_ref.dtype)


def pool_mix(h3, col_block, prefix, n_prev, pool_w, pool_scale):
    b, l, _ = h3.shape
    width = pool_scale.shape[0]
    tc = _tile(l, 256, SUBLANE)
    kernel = functools.partial(_pool_kernel, tc=tc, n_prev=n_prev)
    return pl.pallas_call(
        kernel,
        grid=(b, l // tc),
        in_specs=[pl.BlockSpec((None, tc, width), lambda bi, ti: (bi, ti, col_block)),
                  pl.BlockSpec((None, POOL_HIST, width), lambda bi, ti: (bi, 0, 0)),
                  pl.BlockSpec(pool_w.shape, lambda bi, ti: (0, 0, 0)),
                  pl.BlockSpec((1, width), lambda bi, ti: (0, 0))],
        out_specs=pl.BlockSpec((None, tc, width), lambda bi, ti: (bi, ti, 0)),
        out_shape=jax.ShapeDtypeStruct((b, l, width), BF16),
        scratch_shapes=[pltpu.VMEM((POOL_HIST + tc, width), F32)],
        compiler_params=_params("parallel", "arbitrary"),
        name="pool_mix",
    )(h3, prefix, pool_w, pool_scale.reshape(1, width))


def _kv_export_kernel(*refs, depth, slabs, hd):
    src = refs[:2 * depth]
    ko_ref, vo_ref = refs[2 * depth:]
    d = pl.program_id(0)
    tm = src[0].shape[0]
    for l in range(depth):
        @pl.when(d == l)
        def _layer(l=l):
            for j in range(slabs):
                ko_ref[pl.ds(j, tm, stride=slabs), :] = src[2 * l][:, j * hd:(j + 1) * hd]
                vo_ref[pl.ds(j, tm, stride=slabs), :] = src[2 * l + 1][:, j * hd:(j + 1) * hd]


def kv_export(h_list, attn_w, hd):
    depth = len(h_list)
    m = h_list[0].shape[0]
    slabs = attn_w // hd
    tm = _tile(m, 128, SUBLANE)
    nt = m // tm

    def src_spec(l, col):
        def index(d, i):
            return (jnp.where(d == l, i, jnp.where(d < l, 0, nt - 1)), col)
        return pl.BlockSpec((tm, attn_w), index)

    in_specs, args = [], []
    for l, h in enumerate(h_list):
        in_specs += [src_spec(l, 1), src_spec(l, 2)]
        args += [h, h]
    out_spec = pl.BlockSpec((None, tm * slabs, hd), lambda d, i: (d, i, 0))
    out_shape = jax.ShapeDtypeStruct((depth, m * slabs, hd), F32)
    return pl.pallas_call(
        functools.partial(_kv_export_kernel, depth=depth, slabs=slabs, hd=hd),
        grid=(depth, nt),
        in_specs=in_specs,
        out_specs=[out_spec, out_spec],
        out_shape=[out_shape, out_shape],
        compiler_params=_params("arbitrary", "arbitrary"),
        name="kv_export",
    )(*args)


def _ffn(x, xn, g_post, g_next, wg, wu, wd, lead):
    return resnorm(x, down(gateup(xn, wg, wu, lead), wd, lead), g_post, 0.5, g_next)


def kernel(x_prompt, x_sample, cache_k, cache_v, state_ssm_re, state_ssm_im, state_pool, page_table,
           norm_g, w_ffn_gate, w_ffn_up, w_ffn_down, w_in, w_out, rel_bias, diff_lambda, diff_subln,
           ssm_lam_re, ssm_lam_im, ssm_log_dt, ssm_b_re, ssm_b_im, ssm_c_re, ssm_c_im, ssm_d, ssm_w_glu,
           pool_w, pool_scale):
    bp, seq, d_model = x_prompt.shape
    db, dec_seq, _ = x_sample.shape
    depth = norm_g.shape[0]
    _, n_pool, page, nh, _, hd = cache_k.shape
    vd = 2 * hd
    attn_w = nh * vd
    ssm_w = ssm_d.shape[1]
    pool_wd = pool_scale.shape[1]
    n_groups, n_state_g = ssm_lam_re.shape[1:]
    n_state = n_groups * n_state_g
    n_buf = state_pool.shape[2]
    past_len = page_table.shape[1] * page
    assert attn_w % vd == 0 and ssm_w == vd * (ssm_w // vd) and pool_wd == ssm_w
    ssm_blk = 3 * attn_w // ssm_w
    pool_blk = (3 * attn_w + ssm_w) // pool_wd
    assert page == LANE and 2 * dec_seq == SUBLANE and dec_seq * nh <= LANE
    t_attn = _tile(seq, 512, LANE)
    assert t_attn >= MAX_DISTANCE

    lam_init = [0.8 - 0.6 * math.exp(-0.3 * l) for l in range(depth)]
    lams = diff_lambdas(diff_lambda, lam_init)
    d_tiles, dec_tab = bias_tables(rel_bias, t_attn, dec_seq)
    far = dec_tab[:, 0, 0, 0]
    own_head = jnp.eye(nh, dtype=bool)[:, None, None, :]
    tab = dec_tab[:, :, :dec_seq, :]
    dec_bias = jnp.stack([jnp.where(own_head, tab[:, kind, :, :, None], -jnp.inf)
                          for kind in range(2)]).reshape(2, nh * dec_seq, page * nh)
    dec_bias_new = jnp.where(own_head, tab[:, 2, :, :dec_seq, None], -jnp.inf).reshape(nh * dec_seq, dec_seq * nh)
    dec_bias_new = jnp.pad(dec_bias_new, ((0, 0), (0, LANE - dec_seq * nh)), constant_values=-jnp.inf)
    cache_k4 = cache_k.reshape(depth, n_pool, page * nh * 2, hd)
    cache_v4 = cache_v.reshape(depth, n_pool, page * nh, vd)
    pad_new = ((0, 0), (0, LANE - dec_seq * nh), (0, 0))

    xp = x_prompt.reshape(bp * seq, d_model)
    xs = x_sample.reshape(db * dec_seq, d_model)
    xpn = rmsnorm_cast(xp, norm_g[0, 0])
    xsn = rmsnorm_cast(xs, norm_g[0, 0])
    h_prompt = []
    outs = [[] for _ in range(10)]
    for l in range(depth):
        g = norm_g[l]
        g_after = norm_g[min(l + 1, depth - 1), 0]
        par = jnp.stack([lams[l], jnp.asarray(1.0 - lam_init[l], F32)])
        pw_re, pw_im, bb_re, bb_im = ssm_prep(ssm_lam_re[l], ssm_lam_im[l], ssm_log_dt[l], ssm_b_re[l], ssm_b_im[l])
        b_bd, c_re_bd, c_im_bd = _ssm_blockdiag(bb_re, bb_im, ssm_c_re[l], ssm_c_im[l])
        prep = (b_bd, c_re_bd, c_im_bd, pw_re, pw_im)
        w_glu = ssm_w_glu[l].astype(BF16)
        pw = pool_w[l].astype(BF16)

        xp, xpn = _ffn(xp, xpn, g[1], g[2], w_ffn_gate, w_ffn_up, w_ffn_down, (l, 0))
        h = matmul(xpn, w_in, (l,))
        h_prompt.append(h)
        h3 = h.reshape(bp, seq, -1)
        a_out = attn_prompt(h3, par, far, d_tiles, diff_subln[l], t_attn)
        zeros_state = jnp.zeros((bp, n_state), F32)
        s_out, s_re, s_im = ssm_mix(h3, ssm_blk, seq, zeros_state, zeros_state, prep, ssm_d[l], w_glu)
        p_out = pool_mix(h3, pool_blk, jnp.zeros((bp, POOL_HIST, pool_wd), F32), 0, pw, pool_scale[l])
        mixed = jnp.concatenate([a_out, s_out, p_out], axis=-1).reshape(bp * seq, -1)
        xp, xpn = resnorm(xp, matmul(mixed, w_out, (l,)), g[3], 1.0, g[4])
        xp, xpn = _ffn(xp, xpn, g[5], g_after, w_ffn_gate, w_ffn_up, w_ffn_down, (l, 1))
        outs[2].append(s_re.reshape(bp, n_groups, n_state_g))
        outs[3].append(s_im.reshape(bp, n_groups, n_state_g))
        outs[4].append(h3[:, seq - n_buf:, 3 * attn_w + ssm_w:])

        xs, xsn = _ffn(xs, xsn, g[1], g[2], w_ffn_gate, w_ffn_up, w_ffn_down, (l, 0))
        hs = matmul(xsn, w_in, (l,)).reshape(db, dec_seq, -1)
        q = hs[..., :attn_w].reshape(db, dec_seq, nh, 2, hd).transpose(0, 3, 2, 1, 4)
        q = q.reshape(db, 2, nh * dec_seq, hd).astype(BF16)
        k_new = hs[..., attn_w:2 * attn_w]
        v_new = hs[..., 2 * attn_w:3 * attn_w]
        k_rows = k_new.reshape(db, dec_seq, nh, 2, hd).transpose(0, 3, 1, 2, 4).reshape(db, 2, dec_seq * nh, hd)
        k_rows = jnp.pad(k_rows, ((0, 0),) + pad_new)
        v_rows = jnp.pad(v_new.reshape(db, dec_seq * nh, vd), pad_new)
        a_rows = attn_sample(page_table, par, q, cache_k4, cache_v4, l, k_rows, v_rows, dec_bias, dec_bias_new,
                             diff_subln[l])
        a_s = a_rows.reshape(db, nh, dec_seq, vd).transpose(0, 2, 1, 3).reshape(db, dec_seq, attn_w)
        l_pad = -(-dec_seq // SUBLANE) * SUBLANE
        hs_pad = jnp.pad(hs, ((0, 0), (0, l_pad - dec_seq), (0, 0)))
        s_s, ss_re, ss_im = ssm_mix(hs_pad, ssm_blk, dec_seq, state_ssm_re[l].reshape(db, n_state),
                                    state_ssm_im[l].reshape(db, n_state), prep, ssm_d[l], w_glu)
        prefix = jnp.pad(state_pool[l], ((0, 0), (POOL_HIST - n_buf, 0), (0, 0)))
        p_s = pool_mix(hs_pad, pool_blk, prefix, past_len, pw, pool_scale[l])
        mixed_s = jnp.concatenate([a_s.astype(BF16), s_s[:, :dec_seq], p_s[:, :dec_seq]], axis=-1)
        xs, xsn = resnorm(xs, matmul(mixed_s.reshape(db * dec_seq, -1), w_out, (l,)), g[3], 1.0, g[4])
        xs, xsn = _ffn(xs, xsn, g[5], g_after, w_ffn_gate, w_ffn_up, w_ffn_down, (l, 1))
        up_s = hs[..., 3 * attn_w + ssm_w:]
        outs[5].append(k_new.reshape(db, dec_seq, nh, 2, hd))
        outs[6].append(v_new.reshape(db, dec_seq, nh, vd))
        outs[7].append(ss_re.reshape(db, n_groups, n_state_g))
        outs[8].append(ss_im.reshape(db, n_groups, n_state_g))
        outs[9].append(jnp.concatenate([state_pool[l], up_s], axis=1)[:, -n_buf:])
    k_prompt, v_prompt = kv_export(h_prompt, attn_w, hd)
    return (xp.reshape(bp, seq, d_model), xs.reshape(db, dec_seq, d_model),
            k_prompt.reshape(depth, bp, seq, nh, 2, hd), v_prompt.reshape(depth, bp, seq, nh, vd),
            *[jnp.stack(o) for o in outs[2:]])
```

```python
import functools
import math

import jax
import jax.numpy as jnp
from jax import lax
from jax.experimental import pallas as pl
from jax.experimental.pallas import tpu as pltpu

F32 = jnp.float32
BF16 = jnp.bfloat16

RMS_EPS = 1e-6
N_HEADS = 8
NUM_BUCKETS = 32
MAX_DISTANCE = 128
POOL_WINDOWS = (2, 4, 8, 16)
POOL_HIST = 16
SSM_CH = 16
LANE = 128
SUBLANE = 8
VMEM_LIMIT = 56 * 1024 * 1024
ROW_TILE = 2048
PAGES_PER_STEP = 4
ATTN_ROWS = 64


def _tile(dim, pref, align):
    t = min(pref, dim)
    t -= t % align
    while t >= align:
        if dim % t == 0:
            return t
        t -= align
    return dim


def _params(*sem):
    return pltpu.CompilerParams(dimension_semantics=sem, vmem_limit_bytes=VMEM_LIMIT)


def _rmsnorm_kernel(x_ref, g_ref, o_ref):
    x = x_ref[...]
    ms = jnp.mean(x * x, axis=-1, keepdims=True)
    o_ref[...] = (x * lax.rsqrt(ms + RMS_EPS) * g_ref[...]).astype(o_ref.dtype)


def rmsnorm_cast(x, g):
    m, d = x.shape
    tm = _tile(m, 256, SUBLANE)
    return pl.pallas_call(
        _rmsnorm_kernel,
        grid=(m // tm,),
        in_specs=[pl.BlockSpec((tm, d), lambda i: (i, 0)),
                  pl.BlockSpec((1, d), lambda i: (0, 0))],
        out_specs=pl.BlockSpec((tm, d), lambda i: (i, 0)),
        out_shape=jax.ShapeDtypeStruct((m, d), BF16),
        compiler_params=_params("parallel"),
        name="rmsnorm_cast",
    )(x, g.reshape(1, d))


def _resnorm_kernel(x_ref, y_ref, g_ref, gn_ref, o_ref, on_ref, *, scale):
    y = y_ref[...]
    ms = jnp.mean(y * y, axis=-1, keepdims=True)
    x = x_ref[...] + scale * (y * lax.rsqrt(ms + RMS_EPS) * g_ref[...])
    o_ref[...] = x
    ms = jnp.mean(x * x, axis=-1, keepdims=True)
    on_ref[...] = (x * lax.rsqrt(ms + RMS_EPS) * gn_ref[...]).astype(on_ref.dtype)


def resnorm(x, y, g, scale, g_next):
    m, d = x.shape
    tm = _tile(m, 256, 16)
    row = pl.BlockSpec((tm, d), lambda i: (i, 0))
    vec = pl.BlockSpec((1, d), lambda i: (0, 0))
    return pl.pallas_call(
        functools.partial(_resnorm_kernel, scale=scale),
        grid=(m // tm,),
        in_specs=[row, row, vec, vec],
        out_specs=[row, row],
        out_shape=[jax.ShapeDtypeStruct((m, d), F32), jax.ShapeDtypeStruct((m, d), BF16)],
        compiler_params=_params("parallel"),
        name="resnorm",
    )(x, y, g.reshape(1, d), g_next.reshape(1, d))


def _gateup_kernel(a_ref, wg_ref, wu_ref, o_ref):
    a = a_ref[...]
    g = jnp.dot(a, wg_ref[...].astype(BF16), preferred_element_type=F32)
    u = jnp.dot(a, wu_ref[...].astype(BF16), preferred_element_type=F32)
    o_ref[...] = (g * jax.nn.sigmoid(g) * u).astype(o_ref.dtype)


def _weight_spec(lead, rows, cols, index):
    return pl.BlockSpec((None,) * len(lead) + (rows, cols), lambda i, j: tuple(lead) + index(i, j))


def gateup(a, wg, wu, lead):
    m, k = a.shape
    n = wg.shape[-1]
    tm = _tile(m, ROW_TILE, 16)
    tn = _tile(n, 256, LANE)
    return pl.pallas_call(
        _gateup_kernel,
        grid=(m // tm, n // tn),
        in_specs=[pl.BlockSpec((tm, k), lambda i, j: (i, 0), pipeline_mode=pl.Buffered(1)),
                  _weight_spec(lead, k, tn, lambda i, j: (0, j)),
                  _weight_spec(lead, k, tn, lambda i, j: (0, j))],
        out_specs=pl.BlockSpec((None, tm, tn), lambda i, j: (j, i, 0)),
        out_shape=jax.ShapeDtypeStruct((n // tn, m, tn), BF16),
        compiler_params=_params("parallel", "arbitrary"),
        name="gateup",
    )(a, wg, wu)


def _down_kernel(a_ref, w_ref, o_ref, *, n_chunk):
    k = pl.program_id(1)
    a = a_ref[...]
    n = o_ref.shape[1]

    @pl.when(k == 0)
    def _first():
        for c in range(0, n, n_chunk):
            o_ref[:, c:c + n_chunk] = jnp.dot(a, w_ref[:, c:c + n_chunk].astype(BF16),
                                              preferred_element_type=F32)

    @pl.when(k > 0)
    def _rest():
        for c in range(0, n, n_chunk):
            o_ref[:, c:c + n_chunk] += jnp.dot(a, w_ref[:, c:c + n_chunk].astype(BF16),
                                               preferred_element_type=F32)


def down(act, w, lead):
    nk, m, tk = act.shape
    n = w.shape[-1]
    tm = _tile(m, ROW_TILE, 16)
    n_chunk = _tile(n, 512, LANE)
    return pl.pallas_call(
        functools.partial(_down_kernel, n_chunk=n_chunk),
        grid=(m // tm, nk),
        in_specs=[pl.BlockSpec((None, tm, tk), lambda i, kk: (kk, i, 0)),
                  _weight_spec(lead, tk, n, lambda i, kk: (kk, 0))],
        out_specs=pl.BlockSpec((tm, n), lambda i, kk: (i, 0), pipeline_mode=pl.Buffered(1)),
        out_shape=jax.ShapeDtypeStruct((m, n), F32),
        compiler_params=_params("parallel", "arbitrary"),
        name="down",
    )(act, w)


def _mm_kernel(a_ref, w_ref, o_ref):
    o_ref[...] = jnp.dot(a_ref[...], w_ref[...].astype(BF16), preferred_element_type=F32)


def matmul(a, w, lead):
    m, k = a.shape
    n = w.shape[-1]
    tm = _tile(m, ROW_TILE, 16)
    tn = _tile(n, 512, LANE)
    return pl.pallas_call(
        _mm_kernel,
        grid=(m // tm, n // tn),
        in_specs=[pl.BlockSpec((tm, k), lambda i, j: (i, 0), pipeline_mode=pl.Buffered(1)),
                  _weight_spec(lead, k, tn, lambda i, j: (0, j))],
        out_specs=pl.BlockSpec((tm, tn), lambda i, j: (i, j)),
        out_shape=jax.ShapeDtypeStruct((m, n), F32),
        compiler_params=_params("parallel", "arbitrary"),
        name="matmul",
    )(a, w)


def _t5_bias(rel, rb_ref, h):
    n = jnp.maximum(rel, 0)
    max_exact = NUM_BUCKETS // 2
    nf = jnp.maximum(n, 1).astype(F32)
    large = max_exact + (jnp.log(nf / max_exact) / math.log(MAX_DISTANCE / max_exact)
                         * (NUM_BUCKETS - max_exact)).astype(jnp.int32)
    large = jnp.minimum(large, NUM_BUCKETS - 1)
    bucket = jnp.where(n < max_exact, n, large)
    out = jnp.zeros(rel.shape, F32)
    for b in range(NUM_BUCKETS):
        out = jnp.where(bucket == b, rb_ref[b, h], out)
    return out


def _bias_kernel(rb_ref, d_ref, dec_ref, *, t, dec_seq):
    h = pl.program_id(0)
    r = lax.broadcasted_iota(jnp.int32, (t, t), 0)
    c = lax.broadcasted_iota(jnp.int32, (t, t), 1)
    d_ref[0] = jnp.where(r >= c, _t5_bias(r - c, rb_ref, h), -jnp.inf)
    d_ref[1] = _t5_bias(t + r - c, rb_ref, h)
    row = lax.broadcasted_iota(jnp.int32, (SUBLANE, LANE), 0)
    col = lax.broadcasted_iota(jnp.int32, (SUBLANE, LANE), 1)
    tq = row % dec_seq
    dec_ref[0] = _t5_bias(jnp.full((SUBLANE, LANE), 2 * MAX_DISTANCE, jnp.int32), rb_ref, h)
    dec_ref[1] = _t5_bias(LANE + tq - col, rb_ref, h)
    dec_ref[2] = jnp.where((col <= tq) & (col < dec_seq), _t5_bias(tq - col, rb_ref, h), -jnp.inf)


def bias_tables(rel_bias, t, dec_seq):
    nb, nh = rel_bias.shape
    return pl.pallas_call(
        functools.partial(_bias_kernel, t=t, dec_seq=dec_seq),
        grid=(nh,),
        in_specs=[pl.BlockSpec(memory_space=pltpu.SMEM)],
        out_specs=[pl.BlockSpec((None, 2, t, t), lambda h: (h, 0, 0, 0)),
                   pl.BlockSpec((None, 3, SUBLANE, LANE), lambda h: (h, 0, 0, 0))],
        out_shape=[jax.ShapeDtypeStruct((nh, 2, t, t), F32),
                   jax.ShapeDtypeStruct((nh, 3, SUBLANE, LANE), F32)],
        compiler_params=_params("arbitrary"),
        name="bias_tables",
    )(rel_bias)


def _lambda_kernel(lp_ref, init_ref, o_ref):
    lp = lp_ref[...]
    s1 = jnp.sum(lp[:, 0, :] * lp[:, 1, :], axis=-1, keepdims=True)
    s2 = jnp.sum(lp[:, 2, :] * lp[:, 3, :], axis=-1, keepdims=True)
    o_ref[...] = jnp.broadcast_to(jnp.exp(s1) - jnp.exp(s2) + init_ref[...], o_ref.shape)


def diff_lambdas(diff_lambda, lam_init):
    depth = diff_lambda.shape[0]
    out = pl.pallas_call(
        _lambda_kernel,
        out_shape=jax.ShapeDtypeStruct((depth, LANE), F32),
        name="diff_lambdas",
    )(diff_lambda, jnp.asarray(lam_init, F32).reshape(depth, 1))
    return out[:, 0]


def _attn_kernel(par_ref, far_ref, q_ref, k_ref, v_ref, d_ref, g_ref, o_ref,
                 m_scr, l_scr, acc_scr, s_scr, p_scr, *, scale, hd, rb):
    h = pl.program_id(1)
    qi = pl.program_id(2)
    ki = pl.program_id(3)
    t = q_ref.shape[0]
    n_cb = k_ref.shape[0] // LANE
    n_eb = v_ref.shape[1] // LANE

    @pl.when(ki == 0)
    def _init():
        m_scr[...] = jnp.full(m_scr.shape, -jnp.inf, F32)
        l_scr[...] = jnp.zeros(l_scr.shape, F32)
        acc_scr[...] = jnp.zeros(acc_scr.shape, F32)

    def step(tile):
        q = q_ref[...].astype(BF16)
        k = k_ref[...].astype(BF16)
        v = v_ref[...].astype(BF16)
        for c in range(2):
            s_scr[...] = lax.dot_general(q[:, c * hd:(c + 1) * hd], k[:, c * hd:(c + 1) * hd],
                                         (((1,), (1,)), ((), ())), preferred_element_type=F32)

            def softmax_rows(i, carry, c=c):
                rows = pl.ds(pl.multiple_of(i * rb, rb), rb)
                s = s_scr[rows, :] * scale
                s = s + (far_ref[h] if tile is None else d_ref[tile, rows, :])
                cols = [s[:, j * LANE:(j + 1) * LANE] for j in range(n_cb)]
                m_prev = m_scr[c, rows, :]
                m_new = jnp.maximum(m_prev, jnp.max(functools.reduce(jnp.maximum, cols), axis=-1, keepdims=True))
                alpha = jnp.exp(m_prev - m_new)
                ps = [jnp.exp(col - m_new) for col in cols]
                l_scr[c, rows, :] = alpha * l_scr[c, rows, :] + jnp.sum(functools.reduce(jnp.add, ps),
                                                                       axis=-1, keepdims=True)
                m_scr[c, rows, :] = m_new
                for e in range(n_eb):
                    es = slice(e * LANE, (e + 1) * LANE)
                    acc_scr[c, rows, es] = alpha * acc_scr[c, rows, es]
                for j in range(n_cb):
                    p_scr[rows, j * LANE:(j + 1) * LANE] = ps[j].astype(BF16)
                return carry

            lax.fori_loop(0, t // rb, softmax_rows, 0, unroll=True)
            acc_scr[c] += jnp.dot(p_scr[...], v, preferred_element_type=F32)

    @pl.when(ki == qi)
    def _diag():
        step(0)

    @pl.when(ki == qi - 1)
    def _sub():
        step(1)

    @pl.when(ki < qi - 1)
    def _far():
        step(None)

    @pl.when(ki == qi)
    def _finish():
        o = (acc_scr[0] / l_scr[0][:, 0:1]) - par_ref[0] * (acc_scr[1] / l_scr[1][:, 0:1])
        ms = jnp.mean(o * o, axis=-1, keepdims=True)
        o_ref[...] = ((o * lax.rsqrt(ms + RMS_EPS) * g_ref[...]) * par_ref[1]).astype(o_ref.dtype)


def attn_prompt(h3, par, far, d_tiles, subln_g, t):
    b, s, _ = h3.shape
    nh = d_tiles.shape[0]
    hd = subln_g.shape[0] // 2
    vd = 2 * hd
    nt = s // t
    kernel = functools.partial(_attn_kernel, scale=hd ** -0.5, hd=hd, rb=_tile(t, ATTN_ROWS, 16))
    return pl.pallas_call(
        kernel,
        grid=(b, nh, nt, nt),
        in_specs=[pl.BlockSpec(memory_space=pltpu.SMEM),
                  pl.BlockSpec(memory_space=pltpu.SMEM),
                  pl.BlockSpec((None, t, vd), lambda bi, hi, qi, ki: (bi, qi, hi)),
                  pl.BlockSpec((None, t, vd), lambda bi, hi, qi, ki: (bi, jnp.minimum(ki, qi), nh + hi)),
                  pl.BlockSpec((None, t, vd), lambda bi, hi, qi, ki: (bi, jnp.minimum(ki, qi), 2 * nh + hi)),
                  pl.BlockSpec((None, 2, t, t), lambda bi, hi, qi, ki: (hi, 0, 0, 0)),
                  pl.BlockSpec((1, vd), lambda bi, hi, qi, ki: (0, 0))],
        out_specs=pl.BlockSpec((None, t, vd), lambda bi, hi, qi, ki: (bi, qi, hi)),
        out_shape=jax.ShapeDtypeStruct((b, s, nh * vd), BF16),
        scratch_shapes=[pltpu.VMEM((2, t, LANE), F32), pltpu.VMEM((2, t, LANE), F32),
                        pltpu.VMEM((2, t, vd), F32), pltpu.VMEM((t, t), F32), pltpu.VMEM((t, t), BF16)],
        compiler_params=_params("parallel", "parallel", "parallel", "arbitrary"),
        name="attn_prompt",
    )(par, far, h3, h3, h3, d_tiles, subln_g.reshape(1, vd))


def _dec_attn_kernel(pt_ref, par_ref, q_ref, *refs, scale, n_grp):
    kc_refs = refs[:n_grp]
    vc_refs = refs[n_grp:2 * n_grp]
    kn_ref, vn_ref, bias_ref, bias_new_ref, g_ref, o_ref, m_scr, l_scr, acc_scr = refs[2 * n_grp:]
    p = pl.program_id(1)
    last = pl.num_programs(1) - 1
    n_q = q_ref.shape[1]

    @pl.when(p == 0)
    def _init():
        m_scr[...] = jnp.full(m_scr.shape, -jnp.inf, F32)
        l_scr[...] = jnp.zeros(l_scr.shape, F32)
        acc_scr[...] = jnp.zeros(acc_scr.shape, F32)

    def step(keys, vb, bias):
        s = jnp.concatenate(
            [lax.dot_general(q_ref[c], keys[c], (((1,), (1,)), ((), ())), preferred_element_type=F32) * scale
             + bias for c in range(2)], axis=0)
        m_prev = m_scr[...]
        m_new = jnp.maximum(m_prev, jnp.max(s, axis=-1, keepdims=True))
        alpha = jnp.exp(m_prev - m_new)
        pr = jnp.exp(s - m_new)
        l_scr[...] = alpha * l_scr[...] + jnp.sum(pr, axis=-1, keepdims=True)
        m_scr[...] = m_new
        acc_scr[...] = alpha * acc_scr[...] + jnp.dot(pr.astype(BF16), vb, preferred_element_type=F32)

    def cached_step(j, bias):
        n_kv = vc_refs[j].shape[0]
        keys = [kc_refs[j][pl.ds(c, n_kv, stride=2), :].astype(BF16) for c in range(2)]
        step(keys, vc_refs[j][...].astype(BF16), bias)

    for j in range(n_grp - 1):
        cached_step(j, bias_ref[0])

    @pl.when(p < last)
    def _far():
        cached_step(n_grp - 1, bias_ref[0])

    @pl.when(p == last)
    def _tail():
        cached_step(n_grp - 1, bias_ref[1])
        step([kn_ref[c].astype(BF16) for c in range(2)], vn_ref[...].astype(BF16), bias_new_ref[...])
        o_all = acc_scr[...] / l_scr[...]
        o = o_all[:n_q] - par_ref[0] * o_all[n_q:]
        ms = jnp.mean(o * o, axis=-1, keepdims=True)
        o_ref[...] = (o * lax.rsqrt(ms + RMS_EPS) * g_ref[...]) * par_ref[1]


def attn_sample(page_table, par, q, cache_k, cache_v, layer, k_new, v_new, bias, bias_new, subln_g):
    db, n_pages = page_table.shape
    _, _, k_rows, hd = cache_k.shape
    _, _, v_rows, vd = cache_v.shape
    n_q = q.shape[2]
    n_new = v_new.shape[1]
    n_grp = _tile(n_pages, PAGES_PER_STEP, 1)
    kernel = functools.partial(_dec_attn_kernel, scale=hd ** -0.5, n_grp=n_grp)

    def page_spec(rows, width, j):
        return pl.BlockSpec((None, None, rows, width),
                            lambda bi, pi, pt: (layer, pt[bi * n_pages + pi * n_grp + j], 0, 0))

    grid_spec = pltpu.PrefetchScalarGridSpec(
        num_scalar_prefetch=1,
        grid=(db, n_pages // n_grp),
        in_specs=[pl.BlockSpec(memory_space=pltpu.SMEM),
                  pl.BlockSpec((None, 2, n_q, hd), lambda bi, pi, pt: (bi, 0, 0, 0)),
                  *[page_spec(k_rows, hd, j) for j in range(n_grp)],
                  *[page_spec(v_rows, vd, j) for j in range(n_grp)],
                  pl.BlockSpec((None, 2, n_new, hd), lambda bi, pi, pt: (bi, 0, 0, 0)),
                  pl.BlockSpec((None, n_new, vd), lambda bi, pi, pt: (bi, 0, 0)),
                  pl.BlockSpec((2, n_q, v_rows), lambda bi, pi, pt: (0, 0, 0)),
                  pl.BlockSpec((n_q, n_new), lambda bi, pi, pt: (0, 0)),
                  pl.BlockSpec((1, vd), lambda bi, pi, pt: (0, 0))],
        out_specs=pl.BlockSpec((None, n_q, vd), lambda bi, pi, pt: (bi, 0, 0)),
        scratch_shapes=[pltpu.VMEM((2 * n_q, 1), F32), pltpu.VMEM((2 * n_q, 1), F32),
                        pltpu.VMEM((2 * n_q, vd), F32)],
    )
    return pl.pallas_call(
        kernel,
        grid_spec=grid_spec,
        out_shape=jax.ShapeDtypeStruct((db, n_q, vd), F32),
        compiler_params=_params("parallel", "arbitrary"),
        name="attn_sample",
    )(page_table.reshape(-1), par, q, *([cache_k] * n_grp), *([cache_v] * n_grp), k_new, v_new, bias, bias_new,
      subln_g.reshape(1, vd))


def _ssm_prep_kernel(lr_ref, li_ref, ldt_ref, lrw_ref, liw_ref, br_ref, bi_ref,
                     pw_re_ref, pw_im_ref, bb_re_ref, bb_im_ref):
    dt = jnp.exp(ldt_ref[...])
    zr = lr_ref[...] * dt
    zi = li_ref[...] * dt
    for kk in range(SUBLANE):
        mag = jnp.exp((kk + 1.0) * zr)
        pw_re_ref[kk] = mag * jnp.cos((kk + 1.0) * zi)
        pw_im_ref[kk] = mag * jnp.sin((kk + 1.0) * zi)
    lr = lrw_ref[...]
    li = liw_ref[...]
    mag = jnp.exp(lr * dt)
    x = mag * jnp.cos(li * dt) - 1.0
    y = mag * jnp.sin(li * dt)
    den = lr * lr + li * li
    cr = (x * lr + y * li) / den
    ci = (y * lr - x * li) / den
    br = br_ref[...]
    bi = bi_ref[...]
    bb_re_ref[...] = cr * br - ci * bi
    bb_im_ref[...] = cr * bi + ci * br


def ssm_prep(lam_re, lam_im, log_dt, b_re, b_im):
    g, p = lam_re.shape
    ch = b_re.shape[-1]
    wide = lambda a: jnp.repeat(a, ch, axis=-1)
    outs = pl.pallas_call(
        _ssm_prep_kernel,
        out_shape=[jax.ShapeDtypeStruct((SUBLANE, g, p), F32), jax.ShapeDtypeStruct((SUBLANE, g, p), F32),
                   jax.ShapeDtypeStruct((g, p * ch), F32), jax.ShapeDtypeStruct((g, p * ch), F32)],
        name="ssm_prep",
    )(lam_re, lam_im, log_dt.reshape(g, 1), wide(lam_re), wide(lam_im),
      b_re.reshape(g, p * ch), b_im.reshape(g, p * ch))
    pw_re, pw_im, bb_re, bb_im = outs
    return (pw_re.reshape(SUBLANE, g * p), pw_im.reshape(SUBLANE, g * p),
            bb_re.reshape(g, p, ch), bb_im.reshape(g, p, ch))


def _ssm_blockdiag(bb_re, bb_im, c_re, c_im):
    g, p, ch = bb_re.shape
    gl = LANE // ch
    nj = g // gl
    eye = jnp.eye(gl, dtype=bool)

    def in_proj(bb):
        x = bb.reshape(nj, gl, p, ch).transpose(0, 1, 3, 2)
        x = jnp.where(eye[None, :, None, :, None], x[:, :, :, None, :], 0.0)
        return x.reshape(nj, gl * ch, gl * p)

    def out_proj(c):
        x = c.reshape(nj, gl, ch, p).transpose(0, 1, 3, 2)
        x = jnp.where(eye[None, :, None, :, None], x[:, :, :, None, :], 0.0)
        return x.reshape(nj, gl * p, gl * ch)

    b_bd = jnp.concatenate([in_proj(bb_re), in_proj(bb_im)], axis=-1).astype(BF16)
    return b_bd, out_proj(c_re).astype(BF16), out_proj(c_im).astype(BF16)


def _ssm_kernel(u_ref, s0r_ref, s0i_ref, bbd_ref, cre_ref, cim_ref, pwr_ref, pwi_ref, d_ref, wg_ref,
                o_ref, sr_out_ref, si_out_ref, sre, sim, car_r, car_i, y_scr, *, tc, last_row, scan_w):
    ti = pl.program_id(1)
    nj, cin, two_w = bbd_ref.shape
    w = two_w // 2
    n_state = nj * w

    @pl.when(ti == 0)
    def _load_state():
        car_r[...] = s0r_ref[...]
        car_i[...] = s0i_ref[...]

    u = u_ref[...]
    ub = u.astype(BF16)
    for j in range(nj):
        r = jnp.dot(ub[:, j * cin:(j + 1) * cin], bbd_ref[j], preferred_element_type=F32)
        sre[:, j * w:(j + 1) * w] = r[:, :w]
        sim[:, j * w:(j + 1) * w] = r[:, w:]

    row = lax.broadcasted_iota(jnp.int32, (SUBLANE, scan_w), 0)
    for c0 in range(0, n_state, scan_w):
        cs = slice(c0, c0 + scan_w)
        a_re = [jnp.where(row >= kk + 1, pwr_ref[kk:kk + 1, cs], 0.0) for kk in (0, 1, 3)]
        a_im = [jnp.where(row >= kk + 1, pwi_ref[kk:kk + 1, cs], 0.0) for kk in (0, 1, 3)]
        a8_re = pwr_ref[:, cs]
        a8_im = pwi_ref[:, cs]

        def body(i, carry):
            c_re, c_im = carry
            r0 = pl.multiple_of(i * SUBLANE, SUBLANE)
            x_re = sre[pl.ds(r0, SUBLANE), cs]
            x_im = sim[pl.ds(r0, SUBLANE), cs]
            for step, shift in enumerate((1, 2, 4)):
                s_re = pltpu.roll(x_re, shift, 0)
                s_im = pltpu.roll(x_im, shift, 0)
                x_re, x_im = (x_re + (a_re[step] * s_re - a_im[step] * s_im),
                              x_im + (a_re[step] * s_im + a_im[step] * s_re))
            x_re, x_im = (x_re + (a8_re * c_re - a8_im * c_im),
                          x_im + (a8_re * c_im + a8_im * c_re))
            sre[pl.ds(r0, SUBLANE), cs] = x_re
            sim[pl.ds(r0, SUBLANE), cs] = x_im
            return (jnp.broadcast_to(x_re[SUBLANE - 1:SUBLANE], (SUBLANE, scan_w)),
                    jnp.broadcast_to(x_im[SUBLANE - 1:SUBLANE], (SUBLANE, scan_w)))

        init = (jnp.broadcast_to(car_r[:, cs], (SUBLANE, scan_w)),
                jnp.broadcast_to(car_i[:, cs], (SUBLANE, scan_w)))
        lax.fori_loop(0, tc // SUBLANE, body, init)

    car_r[...] = sre[tc - 1:tc, :]
    car_i[...] = sim[tc - 1:tc, :]

    @pl.when(ti == pl.num_programs(1) - 1)
    def _final_state():
        sr_out_ref[...] = sre[last_row:last_row + 1, :]
        si_out_ref[...] = sim[last_row:last_row + 1, :]

    for j in range(nj):
        y_scr[:, j * cin:(j + 1) * cin] = (
            jnp.dot(sre[:, j * w:(j + 1) * w].astype(BF16), cre_ref[j], preferred_element_type=F32)
            - jnp.dot(sim[:, j * w:(j + 1) * w].astype(BF16), cim_ref[j], preferred_element_type=F32))
    y = y_scr[...] + d_ref[...] * u
    gg = jax.nn.gelu(y)
    gate = jnp.dot(gg.astype(BF16), wg_ref[...], preferred_element_type=F32)
    o_ref[...] = (gg * jax.nn.sigmoid(gate)).astype(o_ref.dtype)


def ssm_mix(h3, col_block, seq_len, s0_re, s0_im, prep, d_skip, w_glu):
    b_bd, c_re_bd, c_im_bd, pw_re, pw_im = prep
    b, l, _ = h3.shape
    nj, cin, two_w = b_bd.shape
    width = nj * cin
    n_state = nj * two_w // 2
    tc = _tile(l, 256, SUBLANE)
    nt = l // tc
    last_row = (seq_len - 1) % tc
    kernel = functools.partial(_ssm_kernel, tc=tc, last_row=last_row, scan_w=256)
    const3 = lambda bi, ti: (0, 0, 0)
    const2 = lambda bi, ti: (0, 0)
    out, s_re, s_im = pl.pallas_call(
        kernel,
        grid=(b, nt),
        in_specs=[pl.BlockSpec((None, tc, width), lambda bi, ti: (bi, ti, col_block)),
                  pl.BlockSpec((None, 1, n_state), lambda bi, ti: (bi, 0, 0)),
                  pl.BlockSpec((None, 1, n_state), lambda bi, ti: (bi, 0, 0)),
                  pl.BlockSpec(b_bd.shape, const3),
                  pl.BlockSpec(c_re_bd.shape, const3),
                  pl.BlockSpec(c_im_bd.shape, const3),
                  pl.BlockSpec(pw_re.shape, const2),
                  pl.BlockSpec(pw_im.shape, const2),
                  pl.BlockSpec((1, width), const2),
                  pl.BlockSpec(w_glu.shape, const2)],
        out_specs=[pl.BlockSpec((None, tc, width), lambda bi, ti: (bi, ti, 0)),
                   pl.BlockSpec((None, 1, n_state), lambda bi, ti: (bi, 0, 0)),
                   pl.BlockSpec((None, 1, n_state), lambda bi, ti: (bi, 0, 0))],
        out_shape=[jax.ShapeDtypeStruct((b, l, width), BF16),
                   jax.ShapeDtypeStruct((b, 1, n_state), F32),
                   jax.ShapeDtypeStruct((b, 1, n_state), F32)],
        scratch_shapes=[pltpu.VMEM((tc, n_state), F32), pltpu.VMEM((tc, n_state), F32),
                        pltpu.VMEM((1, n_state), F32), pltpu.VMEM((1, n_state), F32),
                        pltpu.VMEM((tc, width), F32)],
        compiler_params=_params("parallel", "arbitrary"),
        name="ssm_mix",
    )(h3, s0_re.reshape(b, 1, n_state), s0_im.reshape(b, 1, n_state), b_bd, c_re_bd, c_im_bd,
      pw_re, pw_im, d_skip.reshape(1, width), w_glu)
    return out, s_re.reshape(b, n_state), s_im.reshape(b, n_state)


def _pool_kernel(u_ref, pre_ref, w_ref, sc_ref, o_ref, xp, *, tc, n_prev):
    ti = pl.program_id(1)
    n_win, grp, _ = w_ref.shape

    @pl.when(ti == 0)
    def _prefix():
        xp[0:POOL_HIST, :] = pre_ref[...]

    @pl.when(ti > 0)
    def _history():
        xp[0:POOL_HIST, :] = xp[tc:tc + POOL_HIST, :]

    u = u_ref[...]
    xp[POOL_HIST:POOL_HIST + tc, :] = u
    t_idx = ti * tc + lax.broadcasted_iota(jnp.int32, (tc, 1), 0) + 1
    for gi, win in enumerate(POOL_WINDOWS[:n_win]):
        cs = slice(gi * grp, (gi + 1) * grp)
        acc = u[:, cs]
        for dlt in range(1, win):
            acc = acc + xp[POOL_HIST - dlt:POOL_HIST - dlt + tc, cs]
        count = jnp.minimum(n_prev + t_idx, win).astype(F32)
        m = acc / count - u[:, cs]
        y = jnp.dot(m.astype(BF16), w_ref[gi], preferred_element_type=F32)
        o_ref[:, cs] = (y * sc_ref[:, cs]).astype(o_ref.dtype)


def pool_mix(h3, col_block, prefix, n_prev, pool_w, pool_scale):
    b, l, _ = h3.shape
    width = pool_scale.shape[0]
    tc = _tile(l, 256, SUBLANE)
    kernel = functools.partial(_pool_kernel, tc=tc, n_prev=n_prev)
    return pl.pallas_call(
        kernel,
        grid=(b, l // tc),
        in_specs=[pl.BlockSpec((None, tc, width), lambda bi, ti: (bi, ti, col_block)),
                  pl.BlockSpec((None, POOL_HIST, width), lambda bi, ti: (bi, 0, 0)),
                  pl.BlockSpec(pool_w.shape, lambda bi, ti: (0, 0, 0)),
                  pl.BlockSpec((1, width), lambda bi, ti: (0, 0))],
        out_specs=pl.BlockSpec((None, tc, width), lambda bi, ti: (bi, ti, 0)),
        out_shape=jax.ShapeDtypeStruct((b, l, width), BF16),
        scratch_shapes=[pltpu.VMEM((POOL_HIST + tc, width), F32)],
        compiler_params=_params("parallel", "arbitrary"),
        name="pool_mix",
    )(h3, prefix, pool_w, pool_scale.reshape(1, width))


def _kv_export_kernel(*refs, depth, slabs, hd):
    src = refs[:2 * depth]
    ko_ref, vo_ref = refs[2 * depth:]
    d = pl.program_id(0)
    tm = src[0].shape[0]
    for l in range(depth):
        @pl.when(d == l)
        def _layer(l=l):
            for j in range(slabs):
                ko_ref[pl.ds(j, tm, stride=slabs), :] = src[2 * l][:, j * hd:(j + 1) * hd]
                jv = (j % 2) * (slabs // 2) + j // 2
                vo_ref[pl.ds(jv, tm, stride=slabs), :] = src[2 * l + 1][:, j * hd:(j + 1) * hd]


def kv_export(h_list, attn_w, hd):
    depth = len(h_list)
    m = h_list[0].shape[0]
    slabs = attn_w // hd
    tm = _tile(m, 128, SUBLANE)
    nt = m // tm

    def src_spec(l, col):
        def index(d, i):
            return (jnp.where(d == l, i, jnp.where(d < l, 0, nt - 1)), col)
        return pl.BlockSpec((tm, attn_w), index)

    in_specs, args = [], []
    for l, h in enumerate(h_list):
        in_specs += [src_spec(l, 1), src_spec(l, 2)]
        args += [h, h]
    out_spec = pl.BlockSpec((None, tm * slabs, hd), lambda d, i: (d, i, 0))
    out_shape = jax.ShapeDtypeStruct((depth, m * slabs, hd), F32)
    return pl.pallas_call(
        functools.partial(_kv_export_kernel, depth=depth, slabs=slabs, hd=hd),
        grid=(depth, nt),
        in_specs=in_specs,
        out_specs=[out_spec, out_spec],
        out_shape=[out_shape, out_shape],
        compiler_params=_params("arbitrary", "arbitrary"),
        name="kv_export",
    )(*args)


def _ffn(x, xn, g_post, g_next, wg, wu, wd, lead):
    return resnorm(x, down(gateup(xn, wg, wu, lead), wd, lead), g_post, 0.5, g_next)


def kernel(x_prompt, x_sample, cache_k, cache_v, state_ssm_re, state_ssm_im, state_pool, page_table,
           norm_g, w_ffn_gate, w_ffn_up, w_ffn_down, w_in, w_out, rel_bias, diff_lambda, diff_subln,
           ssm_lam_re, ssm_lam_im, ssm_log_dt, ssm_b_re, ssm_b_im, ssm_c_re, ssm_c_im, ssm_d, ssm_w_glu,
           pool_w, pool_scale):
    bp, seq, d_model = x_prompt.shape
    db, dec_seq, _ = x_sample.shape
    depth = norm_g.shape[0]
    _, n_pool, page, nh, _, hd = cache_k.shape
    vd = 2 * hd
    attn_w = nh * vd
    ssm_w = ssm_d.shape[1]
    pool_wd = pool_scale.shape[1]
    n_groups, n_state_g = ssm_lam_re.shape[1:]
    n_state = n_groups * n_state_g
    n_buf = state_pool.shape[2]
    past_len = page_table.shape[1] * page
    assert attn_w % vd == 0 and ssm_w == vd * (ssm_w // vd) and pool_wd == ssm_w
    ssm_blk = 3 * attn_w // ssm_w
    pool_blk = (3 * attn_w + ssm_w) // pool_wd
    assert page == LANE and 2 * dec_seq == SUBLANE and dec_seq * nh <= LANE
    t_attn = _tile(seq, 512, LANE)
    assert t_attn >= MAX_DISTANCE

    lam_init = [0.8 - 0.6 * math.exp(-0.3 * l) for l in range(depth)]
    lams = diff_lambdas(diff_lambda, lam_init)
    d_tiles, dec_tab = bias_tables(rel_bias, t_attn, dec_seq)
    far = dec_tab[:, 0, 0, 0]
    own_head = jnp.eye(nh, dtype=bool)[:, None, None, :]
    tab = dec_tab[:, :, :dec_seq, :]
    dec_bias = jnp.stack([jnp.where(own_head, tab[:, kind, :, :, None], -jnp.inf)
                          for kind in range(2)]).reshape(2, nh * dec_seq, page * nh)
    dec_bias_new = jnp.where(own_head, tab[:, 2, :, :dec_seq, None], -jnp.inf).reshape(nh * dec_seq, dec_seq * nh)
    dec_bias_new = jnp.pad(dec_bias_new, ((0, 0), (0, LANE - dec_seq * nh)), constant_values=-jnp.inf)
    cache_k4 = cache_k.reshape(depth, n_pool, page * nh * 2, hd)
    cache_v4 = cache_v.reshape(depth, n_pool, page * nh, vd)
    pad_new = ((0, 0), (0, LANE - dec_seq * nh), (0, 0))

    xp = x_prompt.reshape(bp * seq, d_model)
    xs = x_sample.reshape(db * dec_seq, d_model)
    xpn = rmsnorm_cast(xp, norm_g[0, 0])
    xsn = rmsnorm_cast(xs, norm_g[0, 0])
    h_prompt = []
    outs = [[] for _ in range(10)]
    for l in range(depth):
        g = norm_g[l]
        g_after = norm_g[min(l + 1, depth - 1), 0]
        par = jnp.stack([lams[l], jnp.asarray(1.0 - lam_init[l], F32)])
        pw_re, pw_im, bb_re, bb_im = ssm_prep(ssm_lam_re[l], ssm_lam_im[l], ssm_log_dt[l], ssm_b_re[l], ssm_b_im[l])
        b_bd, c_re_bd, c_im_bd = _ssm_blockdiag(bb_re, bb_im, ssm_c_re[l], ssm_c_im[l])
        prep = (b_bd, c_re_bd, c_im_bd, pw_re, pw_im)
        w_glu = ssm_w_glu[l].astype(BF16)
        pw = pool_w[l].astype(BF16)

        xp, xpn = _ffn(xp, xpn, g[1], g[2], w_ffn_gate, w_ffn_up, w_ffn_down, (l, 0))
        h = matmul(xpn, w_in, (l,))
        h_prompt.append(h)
        h3 = h.reshape(bp, seq, -1)
        a_out = attn_prompt(h3, par, far, d_tiles, diff_subln[l], t_attn)
        zeros_state = jnp.zeros((bp, n_state), F32)
        s_out, s_re, s_im = ssm_mix(h3, ssm_blk, seq, zeros_state, zeros_state, prep, ssm_d[l], w_glu)
        p_out = pool_mix(h3, pool_blk, jnp.zeros((bp, POOL_HIST, pool_wd), F32), 0, pw, pool_scale[l])
        mixed = jnp.concatenate([a_out, s_out, p_out], axis=-1).reshape(bp * seq, -1)
        xp, xpn = resnorm(xp, matmul(mixed, w_out, (l,)), g[3], 1.0, g[4])
        xp, xpn = _ffn(xp, xpn, g[5], g_after, w_ffn_gate, w_ffn_up, w_ffn_down, (l, 1))
        outs[2].append(s_re.reshape(bp, n_groups, n_state_g))
        outs[3].append(s_im.reshape(bp, n_groups, n_state_g))
        outs[4].append(h3[:, seq - n_buf:, 3 * attn_w + ssm_w:])

        xs, xsn = _ffn(xs, xsn, g[1], g[2], w_ffn_gate, w_ffn_up, w_ffn_down, (l, 0))
        hs = matmul(xsn, w_in, (l,)).reshape(db, dec_seq, -1)
        q = hs[..., :attn_w].reshape(db, dec_seq, nh, 2, hd).transpose(0, 3, 2, 1, 4)
        q = q.reshape(db, 2, nh * dec_seq, hd).astype(BF16)
        k_new = hs[..., attn_w:2 * attn_w]
        v_new = hs[..., 2 * attn_w:3 * attn_w]
        k_rows = k_new.reshape(db, dec_seq, nh, 2, hd).transpose(0, 3, 1, 2, 4).reshape(db, 2, dec_seq * nh, hd)
        k_rows = jnp.pad(k_rows, ((0, 0),) + pad_new)
        v_rows = jnp.pad(v_new.reshape(db, dec_seq * nh, vd), pad_new)
        a_rows = attn_sample(page_table, par, q, cache_k4, cache_v4, l, k_rows, v_rows, dec_bias, dec_bias_new,
                             diff_subln[l])
        a_s = a_rows.reshape(db, nh, dec_seq, vd).transpose(0, 2, 1, 3).reshape(db, dec_seq, attn_w)
        l_pad = -(-dec_seq // SUBLANE) * SUBLANE
        hs_pad = jnp.pad(hs, ((0, 0), (0, l_pad - dec_seq), (0, 0)))
        s_s, ss_re, ss_im = ssm_mix(hs_pad, ssm_blk, dec_seq, state_ssm_re[l].reshape(db, n_state),
                                    state_ssm_im[l].reshape(db, n_state), prep, ssm_d[l], w_glu)
        prefix = jnp.pad(state_pool[l], ((0, 0), (POOL_HIST - n_buf, 0), (0, 0)))
        p_s = pool_mix(hs_pad, pool_blk, prefix, past_len, pw, pool_scale[l])
        mixed_s = jnp.concatenate([a_s.astype(BF16), s_s[:, :dec_seq], p_s[:, :dec_seq]], axis=-1)
        xs, xsn = resnorm(xs, matmul(mixed_s.reshape(db * dec_seq, -1), w_out, (l,)), g[3], 1.0, g[4])
        xs, xsn = _ffn(xs, xsn, g[5], g_after, w_ffn_gate, w_ffn_up, w_ffn_down, (l, 1))
        up_s = hs[..., 3 * attn_w + ssm_w:]
        outs[5].append(k_new.reshape(db, dec_seq, nh, 2, hd))
        outs[6].append(v_new.reshape(db, dec_seq, nh, vd))
        outs[7].append(ss_re.reshape(db, n_groups, n_state_g))
        outs[8].append(ss_im.reshape(db, n_groups, n_state_g))
        outs[9].append(jnp.concatenate([state_pool[l], up_s], axis=1)[:, -n_buf:])
    k_prompt, v_prompt = kv_export(h_prompt, attn_w, hd)
    return (xp.reshape(bp, seq, d_model), xs.reshape(db, dec_seq, d_model),
            k_prompt.reshape(depth, bp, seq, nh, 2, hd),
            v_prompt.reshape(depth, bp, seq, 2, nh, hd).transpose(0, 1, 2, 4, 3, 5).reshape(depth, bp, seq, nh, vd),
            *[jnp.stack(o) for o in outs[2:]])
```

```python
import functools
import math

import jax
import jax.numpy as jnp
from jax import lax
from jax.experimental import pallas as pl
from jax.experimental.pallas import tpu as pltpu

F32 = jnp.float32
BF16 = jnp.bfloat16

RMS_EPS = 1e-6
N_HEADS = 8
NUM_BUCKETS = 32
MAX_DISTANCE = 128
POOL_WINDOWS = (2, 4, 8, 16)
POOL_HIST = 16
SSM_CH = 16
LANE = 128
SUBLANE = 8
VMEM_LIMIT = 56 * 1024 * 1024
ROW_TILE = 2048
PAGES_PER_STEP = 8
ATTN_ROWS = 64


def _tile(dim, pref, align):
    t = min(pref, dim)
    t -= t % align
    while t >= align:
        if dim % t == 0:
            return t
        t -= align
    return dim


def _params(*sem):
    return pltpu.CompilerParams(dimension_semantics=sem, vmem_limit_bytes=VMEM_LIMIT)


def _rmsnorm_kernel(x_ref, g_ref, o_ref):
    x = x_ref[...]
    ms = jnp.mean(x * x, axis=-1, keepdims=True)
    o_ref[...] = (x * lax.rsqrt(ms + RMS_EPS) * g_ref[...]).astype(o_ref.dtype)


def rmsnorm_cast(x, g):
    m, d = x.shape
    tm = _tile(m, 256, SUBLANE)
    return pl.pallas_call(
        _rmsnorm_kernel,
        grid=(m // tm,),
        in_specs=[pl.BlockSpec((tm, d), lambda i: (i, 0)),
                  pl.BlockSpec((1, d), lambda i: (0, 0))],
        out_specs=pl.BlockSpec((tm, d), lambda i: (i, 0)),
        out_shape=jax.ShapeDtypeStruct((m, d), BF16),
        compiler_params=_params("parallel"),
        name="rmsnorm_cast",
    )(x, g.reshape(1, d))


def _resnorm_kernel(x_ref, y_ref, g_ref, gn_ref, o_ref, on_ref, *, scale):
    y = y_ref[...]
    ms = jnp.mean(y * y, axis=-1, keepdims=True)
    x = x_ref[...] + scale * (y * lax.rsqrt(ms + RMS_EPS) * g_ref[...])
    o_ref[...] = x
    ms = jnp.mean(x * x, axis=-1, keepdims=True)
    on_ref[...] = (x * lax.rsqrt(ms + RMS_EPS) * gn_ref[...]).astype(on_ref.dtype)


def resnorm(x, y, g, scale, g_next):
    m, d = x.shape
    tm = _tile(m, 256, 16)
    row = pl.BlockSpec((tm, d), lambda i: (i, 0))
    vec = pl.BlockSpec((1, d), lambda i: (0, 0))
    return pl.pallas_call(
        functools.partial(_resnorm_kernel, scale=scale),
        grid=(m // tm,),
        in_specs=[row, row, vec, vec],
        out_specs=[row, row],
        out_shape=[jax.ShapeDtypeStruct((m, d), F32), jax.ShapeDtypeStruct((m, d), BF16)],
        compiler_params=_params("parallel"),
        name="resnorm",
    )(x, y, g.reshape(1, d), g_next.reshape(1, d))


def _gateup_kernel(a_ref, wg_ref, wu_ref, o_ref):
    a = a_ref[...]
    g = jnp.dot(a, wg_ref[...].astype(BF16), preferred_element_type=F32)
    u = jnp.dot(a, wu_ref[...].astype(BF16), preferred_element_type=F32)
    o_ref[...] = (g * jax.nn.sigmoid(g) * u).astype(o_ref.dtype)


def _weight_spec(lead, rows, cols, index):
    return pl.BlockSpec((None,) * len(lead) + (rows, cols), lambda i, j: tuple(lead) + index(i, j))


def gateup(a, wg, wu, lead):
    m, k = a.shape
    n = wg.shape[-1]
    tm = _tile(m, ROW_TILE, 16)
    tn = _tile(n, 256, LANE)
    return pl.pallas_call(
        _gateup_kernel,
        grid=(m // tm, n // tn),
        in_specs=[pl.BlockSpec((tm, k), lambda i, j: (i, 0), pipeline_mode=pl.Buffered(1)),
                  _weight_spec(lead, k, tn, lambda i, j: (0, j)),
                  _weight_spec(lead, k, tn, lambda i, j: (0, j))],
        out_specs=pl.BlockSpec((None, tm, tn), lambda i, j: (j, i, 0)),
        out_shape=jax.ShapeDtypeStruct((n // tn, m, tn), BF16),
        compiler_params=_params("parallel", "arbitrary"),
        name="gateup",
    )(a, wg, wu)


def _down_kernel(a_ref, w_ref, o_ref, *, n_chunk):
    k = pl.program_id(1)
    a = a_ref[...]
    n = o_ref.shape[1]

    @pl.when(k == 0)
    def _first():
        for c in range(0, n, n_chunk):
            o_ref[:, c:c + n_chunk] = jnp.dot(a, w_ref[:, c:c + n_chunk].astype(BF16),
                                              preferred_element_type=F32)

    @pl.when(k > 0)
    def _rest():
        for c in range(0, n, n_chunk):
            o_ref[:, c:c + n_chunk] += jnp.dot(a, w_ref[:, c:c + n_chunk].astype(BF16),
                                               preferred_element_type=F32)


def down(act, w, lead):
    nk, m, tk = act.shape
    n = w.shape[-1]
    tm = _tile(m, ROW_TILE, 16)
    n_chunk = _tile(n, 512, LANE)
    return pl.pallas_call(
        functools.partial(_down_kernel, n_chunk=n_chunk),
        grid=(m // tm, nk),
        in_specs=[pl.BlockSpec((None, tm, tk), lambda i, kk: (kk, i, 0)),
                  _weight_spec(lead, tk, n, lambda i, kk: (kk, 0))],
        out_specs=pl.BlockSpec((tm, n), lambda i, kk: (i, 0), pipeline_mode=pl.Buffered(1)),
        out_shape=jax.ShapeDtypeStruct((m, n), F32),
        compiler_params=_params("parallel", "arbitrary"),
        name="down",
    )(act, w)


def _mm_kernel(a_ref, w_ref, o_ref):
    o_ref[...] = jnp.dot(a_ref[...], w_ref[...].astype(BF16), preferred_element_type=F32)


def matmul(a, w, lead):
    m, k = a.shape
    n = w.shape[-1]
    tm = _tile(m, ROW_TILE, 16)
    tn = _tile(n, 512, LANE)
    return pl.pallas_call(
        _mm_kernel,
        grid=(m // tm, n // tn),
        in_specs=[pl.BlockSpec((tm, k), lambda i, j: (i, 0), pipeline_mode=pl.Buffered(1)),
                  _weight_spec(lead, k, tn, lambda i, j: (0, j))],
        out_specs=pl.BlockSpec((tm, tn), lambda i, j: (i, j)),
        out_shape=jax.ShapeDtypeStruct((m, n), F32),
        compiler_params=_params("parallel", "arbitrary"),
        name="matmul",
    )(a, w)


def _mm_split_kernel(*refs):
    a_refs, w_ref, o_ref = refs[:-2], refs[-2], refs[-1]
    wb = w_ref[...].astype(BF16)
    acc, k0 = None, 0
    for a_ref in a_refs:
        kk = a_ref.shape[1]
        part = jnp.dot(a_ref[...], wb[k0:k0 + kk], preferred_element_type=F32)
        acc = part if acc is None else acc + part
        k0 += kk
    o_ref[...] = acc


def matmul_split(parts, w, lead):
    m = parts[0].shape[0]
    k = sum(p.shape[1] for p in parts)
    n = w.shape[-1]
    tm = _tile(m, ROW_TILE, 16)
    tn = _tile(n, 512, LANE)
    return pl.pallas_call(
        _mm_split_kernel,
        grid=(m // tm, n // tn),
        in_specs=[pl.BlockSpec((tm, p.shape[1]), lambda i, j: (i, 0), pipeline_mode=pl.Buffered(1)) for p in parts]
        + [_weight_spec(lead, k, tn, lambda i, j: (0, j))],
        out_specs=pl.BlockSpec((tm, tn), lambda i, j: (i, j)),
        out_shape=jax.ShapeDtypeStruct((m, n), F32),
        compiler_params=_params("parallel", "arbitrary"),
        name="matmul_split",
    )(*parts, w)


def _t5_bias(rel, rb_ref, h):
    n = jnp.maximum(rel, 0)
    max_exact = NUM_BUCKETS // 2
    nf = jnp.maximum(n, 1).astype(F32)
    large = max_exact + (jnp.log(nf / max_exact) / math.log(MAX_DISTANCE / max_exact)
                         * (NUM_BUCKETS - max_exact)).astype(jnp.int32)
    large = jnp.minimum(large, NUM_BUCKETS - 1)
    bucket = jnp.where(n < max_exact, n, large)
    out = jnp.zeros(rel.shape, F32)
    for b in range(NUM_BUCKETS):
        out = jnp.where(bucket == b, rb_ref[b, h], out)
    return out


def _bias_kernel(rb_ref, d_ref, dec_ref, *, t, dec_seq):
    h = pl.program_id(0)
    r = lax.broadcasted_iota(jnp.int32, (t, t), 0)
    c = lax.broadcasted_iota(jnp.int32, (t, t), 1)
    d_ref[0] = jnp.where(r >= c, _t5_bias(r - c, rb_ref, h), -jnp.inf)
    d_ref[1] = _t5_bias(t + r - c, rb_ref, h)
    row = lax.broadcasted_iota(jnp.int32, (SUBLANE, LANE), 0)
    col = lax.broadcasted_iota(jnp.int32, (SUBLANE, LANE), 1)
    tq = row % dec_seq
    dec_ref[0] = _t5_bias(jnp.full((SUBLANE, LANE), 2 * MAX_DISTANCE, jnp.int32), rb_ref, h)
    dec_ref[1] = _t5_bias(LANE + tq - col, rb_ref, h)
    dec_ref[2] = jnp.where((col <= tq) & (col < dec_seq), _t5_bias(tq - col, rb_ref, h), -jnp.inf)


def bias_tables(rel_bias, t, dec_seq):
    nb, nh = rel_bias.shape
    return pl.pallas_call(
        functools.partial(_bias_kernel, t=t, dec_seq=dec_seq),
        grid=(nh,),
        in_specs=[pl.BlockSpec(memory_space=pltpu.SMEM)],
        out_specs=[pl.BlockSpec((None, 2, t, t), lambda h: (h, 0, 0, 0)),
                   pl.BlockSpec((None, 3, SUBLANE, LANE), lambda h: (h, 0, 0, 0))],
        out_shape=[jax.ShapeDtypeStruct((nh, 2, t, t), F32),
                   jax.ShapeDtypeStruct((nh, 3, SUBLANE, LANE), F32)],
        compiler_params=_params("arbitrary"),
        name="bias_tables",
    )(rel_bias)


def _lambda_kernel(lp_ref, init_ref, o_ref):
    lp = lp_ref[...]
    s1 = jnp.sum(lp[:, 0, :] * lp[:, 1, :], axis=-1, keepdims=True)
    s2 = jnp.sum(lp[:, 2, :] * lp[:, 3, :], axis=-1, keepdims=True)
    o_ref[...] = jnp.broadcast_to(jnp.exp(s1) - jnp.exp(s2) + init_ref[...], o_ref.shape)


def diff_lambdas(diff_lambda, lam_init):
    depth = diff_lambda.shape[0]
    out = pl.pallas_call(
        _lambda_kernel,
        out_shape=jax.ShapeDtypeStruct((depth, LANE), F32),
        name="diff_lambdas",
    )(diff_lambda, jnp.asarray(lam_init, F32).reshape(depth, 1))
    return out[:, 0]


def _attn_kernel(par_ref, far_ref, q_ref, k_ref, v_ref, d_ref, g_ref, o_ref,
                 m_scr, l_scr, acc_scr, s_scr, p_scr, *, scale, hd, rb):
    h = pl.program_id(1)
    qi = pl.program_id(2)
    ki = pl.program_id(3)
    t = q_ref.shape[0]
    n_cb = k_ref.shape[0] // LANE
    n_eb = v_ref.shape[1] // LANE

    @pl.when(ki == 0)
    def _init():
        m_scr[...] = jnp.full(m_scr.shape, -jnp.inf, F32)
        l_scr[...] = jnp.zeros(l_scr.shape, F32)
        acc_scr[...] = jnp.zeros(acc_scr.shape, F32)

    def step(tile):
        q = q_ref[...].astype(BF16)
        k = k_ref[...].astype(BF16)
        v = v_ref[...].astype(BF16)
        for c in range(2):
            s_scr[...] = lax.dot_general(q[:, c * hd:(c + 1) * hd], k[:, c * hd:(c + 1) * hd],
                                         (((1,), (1,)), ((), ())), preferred_element_type=F32)

            def softmax_rows(i, carry, c=c):
                rows = pl.ds(pl.multiple_of(i * rb, rb), rb)
                s = s_scr[rows, :] * scale
                s = s + (far_ref[h] if tile is None else d_ref[tile, rows, :])
                cols = [s[:, j * LANE:(j + 1) * LANE] for j in range(n_cb)]
                m_prev = m_scr[c, rows, :]
                m_new = jnp.maximum(m_prev, jnp.max(functools.reduce(jnp.maximum, cols), axis=-1, keepdims=True))
                alpha = jnp.exp(m_prev - m_new)
                ps = [jnp.exp(col - m_new) for col in cols]
                l_scr[c, rows, :] = alpha * l_scr[c, rows, :] + jnp.sum(functools.reduce(jnp.add, ps),
                                                                       axis=-1, keepdims=True)
                m_scr[c, rows, :] = m_new
                for e in range(n_eb):
                    es = slice(e * LANE, (e + 1) * LANE)
                    acc_scr[c, rows, es] = alpha * acc_scr[c, rows, es]
                for j in range(n_cb):
                    p_scr[rows, j * LANE:(j + 1) * LANE] = ps[j].astype(BF16)
                return carry

            lax.fori_loop(0, t // rb, softmax_rows, 0, unroll=True)
            acc_scr[c] += jnp.dot(p_scr[...], v, preferred_element_type=F32)

    @pl.when(ki == qi)
    def _diag():
        step(0)

    @pl.when(ki == qi - 1)
    def _sub():
        step(1)

    @pl.when(ki < qi - 1)
    def _far():
        step(None)

    @pl.when(ki == qi)
    def _finish():
        o = (acc_scr[0] / l_scr[0][:, 0:1]) - par_ref[0] * (acc_scr[1] / l_scr[1][:, 0:1])
        ms = jnp.mean(o * o, axis=-1, keepdims=True)
        o_ref[...] = ((o * lax.rsqrt(ms + RMS_EPS) * g_ref[...]) * par_ref[1]).astype(o_ref.dtype)


def attn_prompt(h3, par, far, d_tiles, subln_g, t):
    b, s, _ = h3.shape
    nh = d_tiles.shape[0]
    hd = subln_g.shape[0] // 2
    vd = 2 * hd
    nt = s // t
    kernel = functools.partial(_attn_kernel, scale=hd ** -0.5, hd=hd, rb=_tile(t, ATTN_ROWS, 16))
    return pl.pallas_call(
        kernel,
        grid=(b, nh, nt, nt),
        in_specs=[pl.BlockSpec(memory_space=pltpu.SMEM),
                  pl.BlockSpec(memory_space=pltpu.SMEM),
                  pl.BlockSpec((None, t, vd), lambda bi, hi, qi, ki: (bi, qi, hi)),
                  pl.BlockSpec((None, t, vd), lambda bi, hi, qi, ki: (bi, jnp.minimum(ki, qi), nh + hi)),
                  pl.BlockSpec((None, t, vd), lambda bi, hi, qi, ki: (bi, jnp.minimum(ki, qi), 2 * nh + hi)),
                  pl.BlockSpec((None, 2, t, t), lambda bi, hi, qi, ki: (hi, 0, 0, 0)),
                  pl.BlockSpec((1, vd), lambda bi, hi, qi, ki: (0, 0))],
        out_specs=pl.BlockSpec((None, t, vd), lambda bi, hi, qi, ki: (bi, qi, hi)),
        out_shape=jax.ShapeDtypeStruct((b, s, nh * vd), BF16),
        scratch_shapes=[pltpu.VMEM((2, t, LANE), F32), pltpu.VMEM((2, t, LANE), F32),
                        pltpu.VMEM((2, t, vd), F32), pltpu.VMEM((t, t), F32), pltpu.VMEM((t, t), BF16)],
        compiler_params=_params("parallel", "parallel", "parallel", "arbitrary"),
        name="attn_prompt",
    )(par, far, h3, h3, h3, d_tiles, subln_g.reshape(1, vd))


def _dec_attn_kernel(pt_ref, par_ref, q_ref, *refs, scale, n_grp):
    kc_refs = refs[:n_grp]
    vc_refs = refs[n_grp:2 * n_grp]
    kn_ref, vn_ref, bias_ref, bias_new_ref, g_ref, o_ref, m_scr, l_scr, acc_scr = refs[2 * n_grp:]
    p = pl.program_id(1)
    last = pl.num_programs(1) - 1
    n_q = q_ref.shape[1]

    @pl.when(p == 0)
    def _init():
        m_scr[...] = jnp.full(m_scr.shape, -jnp.inf, F32)
        l_scr[...] = jnp.zeros(l_scr.shape, F32)
        acc_scr[...] = jnp.zeros(acc_scr.shape, F32)

    def step(keys, vb, bias):
        s = jnp.concatenate(
            [lax.dot_general(q_ref[c], keys[c], (((1,), (1,)), ((), ())), preferred_element_type=F32) * scale
             + bias for c in range(2)], axis=0)
        m_prev = m_scr[...]
        m_new = jnp.maximum(m_prev, jnp.max(s, axis=-1, keepdims=True))
        alpha = jnp.exp(m_prev - m_new)
        pr = jnp.exp(s - m_new)
        l_scr[...] = alpha * l_scr[...] + jnp.sum(pr, axis=-1, keepdims=True)
        m_scr[...] = m_new
        acc_scr[...] = alpha * acc_scr[...] + jnp.dot(pr.astype(BF16), vb, preferred_element_type=F32)

    def cached_step(j, bias):
        n_kv = vc_refs[j].shape[0]
        keys = [kc_refs[j][pl.ds(c, n_kv, stride=2), :].astype(BF16) for c in range(2)]
        step(keys, vc_refs[j][...].astype(BF16), bias)

    for j in range(n_grp - 1):
        cached_step(j, bias_ref[0])

    @pl.when(p < last)
    def _far():
        cached_step(n_grp - 1, bias_ref[0])

    @pl.when(p == last)
    def _tail():
        cached_step(n_grp - 1, bias_ref[1])
        step([kn_ref[c].astype(BF16) for c in range(2)], vn_ref[...].astype(BF16), bias_new_ref[...])
        o_all = acc_scr[...] / l_scr[...]
        o = o_all[:n_q] - par_ref[0] * o_all[n_q:]
        ms = jnp.mean(o * o, axis=-1, keepdims=True)
        o_ref[...] = (o * lax.rsqrt(ms + RMS_EPS) * g_ref[...]) * par_ref[1]


def attn_sample(page_table, par, q, cache_k, cache_v, layer, k_new, v_new, bias, bias_new, subln_g):
    db, n_pages = page_table.shape
    _, _, k_rows, hd = cache_k.shape
    _, _, v_rows, vd = cache_v.shape
    n_q = q.shape[2]
    n_new = v_new.shape[1]
    n_grp = _tile(n_pages, PAGES_PER_STEP, 1)
    kernel = functools.partial(_dec_attn_kernel, scale=hd ** -0.5, n_grp=n_grp)

    def page_spec(rows, width, j):
        return pl.BlockSpec((None, None, rows, width),
                            lambda bi, pi, pt: (layer, pt[bi * n_pages + pi * n_grp + j], 0, 0))

    grid_spec = pltpu.PrefetchScalarGridSpec(
        num_scalar_prefetch=1,
        grid=(db, n_pages // n_grp),
        in_specs=[pl.BlockSpec(memory_space=pltpu.SMEM),
                  pl.BlockSpec((None, 2, n_q, hd), lambda bi, pi, pt: (bi, 0, 0, 0)),
                  *[page_spec(k_rows, hd, j) for j in range(n_grp)],
                  *[page_spec(v_rows, vd, j) for j in range(n_grp)],
                  pl.BlockSpec((None, 2, n_new, hd), lambda bi, pi, pt: (bi, 0, 0, 0)),
                  pl.BlockSpec((None, n_new, vd), lambda bi, pi, pt: (bi, 0, 0)),
                  pl.BlockSpec((2, n_q, v_rows), lambda bi, pi, pt: (0, 0, 0)),
                  pl.BlockSpec((n_q, n_new), lambda bi, pi, pt: (0, 0)),
                  pl.BlockSpec((1, vd), lambda bi, pi, pt: (0, 0))],
        out_specs=pl.BlockSpec((None, n_q, vd), lambda bi, pi, pt: (bi, 0, 0)),
        scratch_shapes=[pltpu.VMEM((2 * n_q, 1), F32), pltpu.VMEM((2 * n_q, 1), F32),
                        pltpu.VMEM((2 * n_q, vd), F32)],
    )
    return pl.pallas_call(
        kernel,
        grid_spec=grid_spec,
        out_shape=jax.ShapeDtypeStruct((db, n_q, vd), F32),
        compiler_params=_params("parallel", "arbitrary"),
        name="attn_sample",
    )(page_table.reshape(-1), par, q, *([cache_k] * n_grp), *([cache_v] * n_grp), k_new, v_new, bias, bias_new,
      subln_g.reshape(1, vd))


def _ssm_prep_kernel(lr_ref, li_ref, ldt_ref, lrw_ref, liw_ref, br_ref, bi_ref,
                     pw_re_ref, pw_im_ref, bb_re_ref, bb_im_ref):
    dt = jnp.exp(ldt_ref[...])
    zr = lr_ref[...] * dt
    zi = li_ref[...] * dt
    for kk in range(SUBLANE):
        mag = jnp.exp((kk + 1.0) * zr)
        pw_re_ref[kk] = mag * jnp.cos((kk + 1.0) * zi)
        pw_im_ref[kk] = mag * jnp.sin((kk + 1.0) * zi)
    lr = lrw_ref[...]
    li = liw_ref[...]
    mag = jnp.exp(lr * dt)
    x = mag * jnp.cos(li * dt) - 1.0
    y = mag * jnp.sin(li * dt)
    den = lr * lr + li * li
    cr = (x * lr + y * li) / den
    ci = (y * lr - x * li) / den
    br = br_ref[...]
    bi = bi_ref[...]
    bb_re_ref[...] = cr * br - ci * bi
    bb_im_ref[...] = cr * bi + ci * br


def ssm_prep(lam_re, lam_im, log_dt, b_re, b_im):
    g, p = lam_re.shape
    ch = b_re.shape[-1]
    wide = lambda a: jnp.repeat(a, ch, axis=-1)
    outs = pl.pallas_call(
        _ssm_prep_kernel,
        out_shape=[jax.ShapeDtypeStruct((SUBLANE, g, p), F32), jax.ShapeDtypeStruct((SUBLANE, g, p), F32),
                   jax.ShapeDtypeStruct((g, p * ch), F32), jax.ShapeDtypeStruct((g, p * ch), F32)],
        name="ssm_prep",
    )(lam_re, lam_im, log_dt.reshape(g, 1), wide(lam_re), wide(lam_im),
      b_re.reshape(g, p * ch), b_im.reshape(g, p * ch))
    pw_re, pw_im, bb_re, bb_im = outs
    return (pw_re.reshape(SUBLANE, g * p), pw_im.reshape(SUBLANE, g * p),
            bb_re.reshape(g, p, ch), bb_im.reshape(g, p, ch))


def _ssm_blockdiag(bb_re, bb_im, c_re, c_im):
    g, p, ch = bb_re.shape
    gl = LANE // ch
    nj = g // gl
    eye = jnp.eye(gl, dtype=bool)

    def in_proj(bb):
        x = bb.reshape(nj, gl, p, ch).transpose(0, 1, 3, 2)
        x = jnp.where(eye[None, :, None, :, None], x[:, :, :, None, :], 0.0)
        return x.reshape(nj, gl * ch, gl * p)

    def out_proj(c):
        x = c.reshape(nj, gl, ch, p).transpose(0, 1, 3, 2)
        x = jnp.where(eye[None, :, None, :, None], x[:, :, :, None, :], 0.0)
        return x.reshape(nj, gl * p, gl * ch)

    b_bd = jnp.concatenate([in_proj(bb_re), in_proj(bb_im)], axis=-1).astype(BF16)
    return b_bd, out_proj(c_re).astype(BF16), out_proj(c_im).astype(BF16)


def _ssm_kernel(u_ref, s0r_ref, s0i_ref, bbd_ref, cre_ref, cim_ref, pwr_ref, pwi_ref, d_ref, wg_ref,
                o_ref, sr_out_ref, si_out_ref, sre, sim, car_r, car_i, y_scr, *, tc, last_row, scan_w):
    ti = pl.program_id(1)
    nj, cin, two_w = bbd_ref.shape
    w = two_w // 2
    n_state = nj * w

    @pl.when(ti == 0)
    def _load_state():
        car_r[...] = s0r_ref[...]
        car_i[...] = s0i_ref[...]

    u = u_ref[...]
    ub = u.astype(BF16)
    for j in range(nj):
        r = jnp.dot(ub[:, j * cin:(j + 1) * cin], bbd_ref[j], preferred_element_type=F32)
        sre[:, j * w:(j + 1) * w] = r[:, :w]
        sim[:, j * w:(j + 1) * w] = r[:, w:]

    row = lax.broadcasted_iota(jnp.int32, (SUBLANE, scan_w), 0)
    for c0 in range(0, n_state, scan_w):
        cs = slice(c0, c0 + scan_w)
        a_re = [jnp.where(row >= kk + 1, pwr_ref[kk:kk + 1, cs], 0.0) for kk in (0, 1, 3)]
        a_im = [jnp.where(row >= kk + 1, pwi_ref[kk:kk + 1, cs], 0.0) for kk in (0, 1, 3)]
        a8_re = pwr_ref[:, cs]
        a8_im = pwi_ref[:, cs]

        def body(i, carry):
            c_re, c_im = carry
            r0 = pl.multiple_of(i * SUBLANE, SUBLANE)
            x_re = sre[pl.ds(r0, SUBLANE), cs]
            x_im = sim[pl.ds(r0, SUBLANE), cs]
            for step, shift in enumerate((1, 2, 4)):
                s_re = pltpu.roll(x_re, shift, 0)
                s_im = pltpu.roll(x_im, shift, 0)
                x_re, x_im = (x_re + (a_re[step] * s_re - a_im[step] * s_im),
                              x_im + (a_re[step] * s_im + a_im[step] * s_re))
            x_re, x_im = (x_re + (a8_re * c_re - a8_im * c_im),
                          x_im + (a8_re * c_im + a8_im * c_re))
            sre[pl.ds(r0, SUBLANE), cs] = x_re
            sim[pl.ds(r0, SUBLANE), cs] = x_im
            return (jnp.broadcast_to(x_re[SUBLANE - 1:SUBLANE], (SUBLANE, scan_w)),
                    jnp.broadcast_to(x_im[SUBLANE - 1:SUBLANE], (SUBLANE, scan_w)))

        init = (jnp.broadcast_to(car_r[:, cs], (SUBLANE, scan_w)),
                jnp.broadcast_to(car_i[:, cs], (SUBLANE, scan_w)))
        lax.fori_loop(0, tc // SUBLANE, body, init)

    car_r[...] = sre[tc - 1:tc, :]
    car_i[...] = sim[tc - 1:tc, :]

    @pl.when(ti == pl.num_programs(1) - 1)
    def _final_state():
        sr_out_ref[...] = sre[last_row:last_row + 1, :]
        si_out_ref[...] = sim[last_row:last_row + 1, :]

    for j in range(nj):
        y_scr[:, j * cin:(j + 1) * cin] = (
            jnp.dot(sre[:, j * w:(j + 1) * w].astype(BF16), cre_ref[j], preferred_element_type=F32)
            - jnp.dot(sim[:, j * w:(j + 1) * w].astype(BF16), cim_ref[j], preferred_element_type=F32))
    y = y_scr[...] + d_ref[...] * u
    gg = jax.nn.gelu(y)
    gate = jnp.dot(gg.astype(BF16), wg_ref[...], preferred_element_type=F32)
    o_ref[...] = (gg * jax.nn.sigmoid(gate)).astype(o_ref.dtype)


def ssm_mix(h3, col_block, seq_len, s0_re, s0_im, prep, d_skip, w_glu):
    b_bd, c_re_bd, c_im_bd, pw_re, pw_im = prep
    b, l, _ = h3.shape
    nj, cin, two_w = b_bd.shape
    width = nj * cin
    n_state = nj * two_w // 2
    tc = _tile(l, 256, SUBLANE)
    nt = l // tc
    last_row = (seq_len - 1) % tc
    kernel = functools.partial(_ssm_kernel, tc=tc, last_row=last_row, scan_w=256)
    const3 = lambda bi, ti: (0, 0, 0)
    const2 = lambda bi, ti: (0, 0)
    out, s_re, s_im = pl.pallas_call(
        kernel,
        grid=(b, nt),
        in_specs=[pl.BlockSpec((None, tc, width), lambda bi, ti: (bi, ti, col_block)),
                  pl.BlockSpec((None, 1, n_state), lambda bi, ti: (bi, 0, 0)),
                  pl.BlockSpec((None, 1, n_state), lambda bi, ti: (bi, 0, 0)),
                  pl.BlockSpec(b_bd.shape, const3),
                  pl.BlockSpec(c_re_bd.shape, const3),
                  pl.BlockSpec(c_im_bd.shape, const3),
                  pl.BlockSpec(pw_re.shape, const2),
                  pl.BlockSpec(pw_im.shape, const2),
                  pl.BlockSpec((1, width), const2),
                  pl.BlockSpec(w_glu.shape, const2)],
        out_specs=[pl.BlockSpec((None, tc, width), lambda bi, ti: (bi, ti, 0)),
                   pl.BlockSpec((None, 1, n_state), lambda bi, ti: (bi, 0, 0)),
                   pl.BlockSpec((None, 1, n_state), lambda bi, ti: (bi, 0, 0))],
        out_shape=[jax.ShapeDtypeStruct((b, l, width), BF16),
                   jax.ShapeDtypeStruct((b, 1, n_state), F32),
                   jax.ShapeDtypeStruct((b, 1, n_state), F32)],
        scratch_shapes=[pltpu.VMEM((tc, n_state), F32), pltpu.VMEM((tc, n_state), F32),
                        pltpu.VMEM((1, n_state), F32), pltpu.VMEM((1, n_state), F32),
                        pltpu.VMEM((tc, width), F32)],
        compiler_params=_params("parallel", "arbitrary"),
        name="ssm_mix",
    )(h3, s0_re.reshape(b, 1, n_state), s0_im.reshape(b, 1, n_state), b_bd, c_re_bd, c_im_bd,
      pw_re, pw_im, d_skip.reshape(1, width), w_glu)
    return out, s_re.reshape(b, n_state), s_im.reshape(b, n_state)


def _pool_kernel(u_ref, pre_ref, w_ref, sc_ref, o_ref, xp, *, tc, n_prev):
    ti = pl.program_id(1)
    n_win, grp, _ = w_ref.shape

    @pl.when(ti == 0)
    def _prefix():
        xp[0:POOL_HIST, :] = pre_ref[...]

    @pl.when(ti > 0)
    def _history():
        xp[0:POOL_HIST, :] = xp[tc:tc + POOL_HIST, :]

    u = u_ref[...]
    xp[POOL_HIST:POOL_HIST + tc, :] = u
    t_idx = ti * tc + lax.broadcasted_iota(jnp.int32, (tc, 1), 0) + 1
    for gi, win in enumerate(POOL_WINDOWS[:n_win]):
        cs = slice(gi * grp, (gi + 1) * grp)
        acc = u[:, cs]
        for dlt in range(1, win):
            acc = acc + xp[POOL_HIST - dlt:POOL_HIST - dlt + tc, cs]
        count = jnp.minimum(n_prev + t_idx, win).astype(F32)
        m = acc / count - u[:, cs]
        y = jnp.dot(m.astype(BF16), w_ref[gi], preferred_element_type=F32)
        o_ref[:, cs] = (y * sc_ref[:, cs]).astype(o_ref.dtype)


def pool_mix(h3, col_block, prefix, n_prev, pool_w, pool_scale):
    b, l, _ = h3.shape
    width = pool_scale.shape[0]
    tc = _tile(l, 256, SUBLANE)
    kernel = functools.partial(_pool_kernel, tc=tc, n_prev=n_prev)
    return pl.pallas_call(
        kernel,
        grid=(b, l // tc),
        in_specs=[pl.BlockSpec((None, tc, width), lambda bi, ti: (bi, ti, col_block)),
                  pl.BlockSpec((None, POOL_HIST, width), lambda bi, ti: (bi, 0, 0)),
                  pl.BlockSpec(pool_w.shape, lambda bi, ti: (0, 0, 0)),
                  pl.BlockSpec((1, width), lambda bi, ti: (0, 0))],
        out_specs=pl.BlockSpec((None, tc, width), lambda bi, ti: (bi, ti, 0)),
        out_shape=jax.ShapeDtypeStruct((b, l, width), BF16),
        scratch_shapes=[pltpu.VMEM((POOL_HIST + tc, width), F32)],
        compiler_params=_params("parallel", "arbitrary"),
        name="pool_mix",
    )(h3, prefix, pool_w, pool_scale.reshape(1, width))


def _kv_export_kernel(*refs, depth, slabs, hd):
    src = refs[:2 * depth]
    ko_ref, vo_ref = refs[2 * depth:]
    d = pl.program_id(0)
    tm = src[0].shape[0]
    for l in range(depth):
        @pl.when(d == l)
        def _layer(l=l):
            for j in range(slabs):
                ko_ref[pl.ds(j, tm, stride=slabs), :] = src[2 * l][:, j * hd:(j + 1) * hd]
                jv = (j % 2) * (slabs // 2) + j // 2
                vo_ref[pl.ds(jv, tm, stride=slabs), :] = src[2 * l + 1][:, j * hd:(j + 1) * hd]


def kv_export(h_list, attn_w, hd):
    depth = len(h_list)
    m = h_list[0].shape[0]
    slabs = attn_w // hd
    tm = _tile(m, 128, SUBLANE)
    nt = m // tm

    def src_spec(l, col):
        def index(d, i):
            return (jnp.where(d == l, i, jnp.where(d < l, 0, nt - 1)), col)
        return pl.BlockSpec((tm, attn_w), index)

    in_specs, args = [], []
    for l, h in enumerate(h_list):
        in_specs += [src_spec(l, 1), src_spec(l, 2)]
        args += [h, h]
    out_spec = pl.BlockSpec((None, tm * slabs, hd), lambda d, i: (d, i, 0))
    out_shape = jax.ShapeDtypeStruct((depth, m * slabs, hd), F32)
    return pl.pallas_call(
        functools.partial(_kv_export_kernel, depth=depth, slabs=slabs, hd=hd),
        grid=(depth, nt),
        in_specs=in_specs,
        out_specs=[out_spec, out_spec],
        out_shape=[out_shape, out_shape],
        compiler_params=_params("arbitrary", "arbitrary"),
        name="kv_export",
    )(*args)


def _ffn(x, xn, g_post, g_next, wg, wu, wd, lead):
    return resnorm(x, down(gateup(xn, wg, wu, lead), wd, lead), g_post, 0.5, g_next)


def kernel(x_prompt, x_sample, cache_k, cache_v, state_ssm_re, state_ssm_im, state_pool, page_table,
           norm_g, w_ffn_gate, w_ffn_up, w_ffn_down, w_in, w_out, rel_bias, diff_lambda, diff_subln,
           ssm_lam_re, ssm_lam_im, ssm_log_dt, ssm_b_re, ssm_b_im, ssm_c_re, ssm_c_im, ssm_d, ssm_w_glu,
           pool_w, pool_scale):
    bp, seq, d_model = x_prompt.shape
    db, dec_seq, _ = x_sample.shape
    depth = norm_g.shape[0]
    _, n_pool, page, nh, _, hd = cache_k.shape
    vd = 2 * hd
    attn_w = nh * vd
    ssm_w = ssm_d.shape[1]
    pool_wd = pool_scale.shape[1]
    n_groups, n_state_g = ssm_lam_re.shape[1:]
    n_state = n_groups * n_state_g
    n_buf = state_pool.shape[2]
    past_len = page_table.shape[1] * page
    assert attn_w % vd == 0 and ssm_w == vd * (ssm_w // vd) and pool_wd == ssm_w
    ssm_blk = 3 * attn_w // ssm_w
    pool_blk = (3 * attn_w + ssm_w) // pool_wd
    assert page == LANE and 2 * dec_seq == SUBLANE and dec_seq * nh <= LANE
    t_attn = _tile(seq, 512, LANE)
    assert t_attn >= MAX_DISTANCE

    lam_init = [0.8 - 0.6 * math.exp(-0.3 * l) for l in range(depth)]
    lams = diff_lambdas(diff_lambda, lam_init)
    d_tiles, dec_tab = bias_tables(rel_bias, t_attn, dec_seq)
    far = dec_tab[:, 0, 0, 0]
    own_head = jnp.eye(nh, dtype=bool)[:, None, None, :]
    tab = dec_tab[:, :, :dec_seq, :]
    dec_bias = jnp.stack([jnp.where(own_head, tab[:, kind, :, :, None], -jnp.inf)
                          for kind in range(2)]).reshape(2, nh * dec_seq, page * nh)
    dec_bias_new = jnp.where(own_head, tab[:, 2, :, :dec_seq, None], -jnp.inf).reshape(nh * dec_seq, dec_seq * nh)
    dec_bias_new = jnp.pad(dec_bias_new, ((0, 0), (0, LANE - dec_seq * nh)), constant_values=-jnp.inf)
    cache_k4 = cache_k.reshape(depth, n_pool, page * nh * 2, hd)
    cache_v4 = cache_v.reshape(depth, n_pool, page * nh, vd)
    pad_new = ((0, 0), (0, LANE - dec_seq * nh), (0, 0))

    xp = x_prompt.reshape(bp * seq, d_model)
    xs = x_sample.reshape(db * dec_seq, d_model)
    xpn = rmsnorm_cast(xp, norm_g[0, 0])
    xsn = rmsnorm_cast(xs, norm_g[0, 0])
    h_prompt = []
    outs = [[] for _ in range(10)]
    for l in range(depth):
        g = norm_g[l]
        g_after = norm_g[min(l + 1, depth - 1), 0]
        par = jnp.stack([lams[l], jnp.asarray(1.0 - lam_init[l], F32)])
        pw_re, pw_im, bb_re, bb_im = ssm_prep(ssm_lam_re[l], ssm_lam_im[l], ssm_log_dt[l], ssm_b_re[l], ssm_b_im[l])
        b_bd, c_re_bd, c_im_bd = _ssm_blockdiag(bb_re, bb_im, ssm_c_re[l], ssm_c_im[l])
        prep = (b_bd, c_re_bd, c_im_bd, pw_re, pw_im)
        w_glu = ssm_w_glu[l].astype(BF16)
        pw = pool_w[l].astype(BF16)

        xp, xpn = _ffn(xp, xpn, g[1], g[2], w_ffn_gate, w_ffn_up, w_ffn_down, (l, 0))
        h = matmul(xpn, w_in, (l,))
        h_prompt.append(h)
        h3 = h.reshape(bp, seq, -1)
        a_out = attn_prompt(h3, par, far, d_tiles, diff_subln[l], t_attn)
        zeros_state = jnp.zeros((bp, n_state), F32)
        s_out, s_re, s_im = ssm_mix(h3, ssm_blk, seq, zeros_state, zeros_state, prep, ssm_d[l], w_glu)
        p_out = pool_mix(h3, pool_blk, jnp.zeros((bp, POOL_HIST, pool_wd), F32), 0, pw, pool_scale[l])
        heads = [o.reshape(bp * seq, -1) for o in (a_out, s_out, p_out)]
        xp, xpn = resnorm(xp, matmul_split(heads, w_out, (l,)), g[3], 1.0, g[4])
        xp, xpn = _ffn(xp, xpn, g[5], g_after, w_ffn_gate, w_ffn_up, w_ffn_down, (l, 1))
        outs[2].append(s_re.reshape(bp, n_groups, n_state_g))
        outs[3].append(s_im.reshape(bp, n_groups, n_state_g))
        outs[4].append(h3[:, seq - n_buf:, 3 * attn_w + ssm_w:])

        xs, xsn = _ffn(xs, xsn, g[1], g[2], w_ffn_gate, w_ffn_up, w_ffn_down, (l, 0))
        hs = matmul(xsn, w_in, (l,)).reshape(db, dec_seq, -1)
        q = hs[..., :attn_w].reshape(db, dec_seq, nh, 2, hd).transpose(0, 3, 2, 1, 4)
        q = q.reshape(db, 2, nh * dec_seq, hd).astype(BF16)
        k_new = hs[..., attn_w:2 * attn_w]
        v_new = hs[..., 2 * attn_w:3 * attn_w]
        k_rows = k_new.reshape(db, dec_seq, nh, 2, hd).transpose(0, 3, 1, 2, 4).reshape(db, 2, dec_seq * nh, hd)
        k_rows = jnp.pad(k_rows, ((0, 0),) + pad_new)
        v_rows = jnp.pad(v_new.reshape(db, dec_seq * nh, vd), pad_new)
        a_rows = attn_sample(page_table, par, q, cache_k4, cache_v4, l, k_rows, v_rows, dec_bias, dec_bias_new,
                             diff_subln[l])
        a_s = a_rows.reshape(db, nh, dec_seq, vd).transpose(0, 2, 1, 3).reshape(db, dec_seq, attn_w)
        l_pad = -(-dec_seq // SUBLANE) * SUBLANE
        hs_pad = jnp.pad(hs, ((0, 0), (0, l_pad - dec_seq), (0, 0)))
        s_s, ss_re, ss_im = ssm_mix(hs_pad, ssm_blk, dec_seq, state_ssm_re[l].reshape(db, n_state),
                                    state_ssm_im[l].reshape(db, n_state), prep, ssm_d[l], w_glu)
        prefix = jnp.pad(state_pool[l], ((0, 0), (POOL_HIST - n_buf, 0), (0, 0)))
        p_s = pool_mix(hs_pad, pool_blk, prefix, past_len, pw, pool_scale[l])
        mixed_s = jnp.concatenate([a_s.astype(BF16), s_s[:, :dec_seq], p_s[:, :dec_seq]], axis=-1)
        xs, xsn = resnorm(xs, matmul(mixed_s.reshape(db * dec_seq, -1), w_out, (l,)), g[3], 1.0, g[4])
        xs, xsn = _ffn(xs, xsn, g[5], g_after, w_ffn_gate, w_ffn_up, w_ffn_down, (l, 1))
        up_s = hs[..., 3 * attn_w + ssm_w:]
        outs[5].append(k_new.reshape(db, dec_seq, nh, 2, hd))
        outs[6].append(v_new.reshape(db, dec_seq, nh, vd))
        outs[7].append(ss_re.reshape(db, n_groups, n_state_g))
        outs[8].append(ss_im.reshape(db, n_groups, n_state_g))
        outs[9].append(jnp.concatenate([state_pool[l], up_s], axis=1)[:, -n_buf:])
    k_prompt, v_prompt = kv_export(h_prompt, attn_w, hd)
    return (xp.reshape(bp, seq, d_model), xs.reshape(db, dec_seq, d_model),
            k_prompt.reshape(depth, bp, seq, nh, 2, hd),
            v_prompt.reshape(depth, bp, seq, 2, nh, hd).transpose(0, 1, 2, 4, 3, 5).reshape(depth, bp, seq, nh, vd),
            *[jnp.stack(o) for o in outs[2:]])
```
